```python
import math
import jax, jax.numpy as jnp
from jax import lax
import numpy as np

D_MODEL = 4096
BATCH = 1
SEQ = 8192
DEPTH = 2

GRID_W = 64
CTX_LEN = 256
CONV_CH = D_MODEL // 4
NA_WIDTH = D_MODEL // 2
SSM_CH = D_MODEL - CONV_CH - NA_WIDTH
MIX_WIDTH = CONV_CH + NA_WIDTH + SSM_CH
IN_WIDTH = 2 * CONV_CH + 3 * NA_WIDTH + SSM_CH
NA_HEAD_DIM = 128
NA_HEADS = NA_WIDTH // NA_HEAD_DIM
WIN_H = 8
WIN_W = 16
N_COL_BLOCKS = GRID_W // WIN_W
KEY_BLOCK_W = 2 * WIN_W
CONV_K = 31
SSM_GROUP = 16
SSM_GROUPS = SSM_CH // SSM_GROUP
SSM_STATE = 64
N_DIR = 2
D_FF = -((-8 * D_MODEL) // (3 * 256)) * 256
EPS = 1e-6
NEG_INF = -1e30

kernel_name = 'hybrid_conformer_natten_s5_dit'


def _rms_norm(x, g):
    xf = x.astype(jnp.float32)
    y = xf * lax.rsqrt(jnp.mean(xf * xf, axis=-1, keepdims=True) + EPS)
    return (y * g.astype(jnp.float32)).astype(x.dtype)


def _modulate(h, shift, scale):
    return h * (1 + scale) + shift


def _swiglu(h, w1, w3, w2):
    return (jax.nn.silu(h @ w1) * (h @ w3)) @ w2


def _split_heads(t):
    return t.reshape(t.shape[0], t.shape[1], NA_HEADS, NA_HEAD_DIM)


def _conv_module(a, conv_w, conv_b, ln_g, ln_b, pw_w, pw_b):
    u = a[..., :CONV_CH] * jax.nn.sigmoid(a[..., CONV_CH:])
    y = lax.conv_general_dilated(u, conv_w[:, None, :], window_strides=(1,),
                                 padding=[(CONV_K // 2, CONV_K // 2)],
                                 dimension_numbers=('NWC', 'WIO', 'NWC'),
                                 feature_group_count=CONV_CH) + conv_b
    yf = y.astype(jnp.float32)
    mu = jnp.mean(yf, axis=-1, keepdims=True)
    var = jnp.mean(jnp.square(yf - mu), axis=-1, keepdims=True)
    yn = (yf - mu) * lax.rsqrt(var + EPS) * ln_g.astype(jnp.float32) + ln_b.astype(jnp.float32)
    y = jax.nn.silu(yn).astype(a.dtype)
    return y @ pw_w + pw_b


def _col_tables():
    starts = np.clip(np.arange(N_COL_BLOCKS) * WIN_W - WIN_W // 2, 0, GRID_W - KEY_BLOCK_W)
    qcol = np.arange(N_COL_BLOCKS)[:, None] * WIN_W + np.arange(WIN_W)[None, :]
    kcol = starts[:, None] + np.arange(KEY_BLOCK_W)[None, :]
    qstart = np.clip(qcol - WIN_W // 2, 0, GRID_W - WIN_W)
    valid = (kcol[:, None, :] >= qstart[:, :, None]) & (kcol[:, None, :] < qstart[:, :, None] + WIN_W)
    dc = np.clip(kcol[:, None, :] - qcol[:, :, None] + WIN_W - 1, 0, 2 * WIN_W - 2)
    return [int(s) for s in starts], valid, dc


def _na_latent(q, k, v, kc, vc, rpb, rows):
    b = q.shape[0]
    kh = min(WIN_H, rows)
    n_loc = kh * KEY_BLOCK_W
    starts, valid, dc = _col_tables()
    mask = jnp.asarray(np.broadcast_to(valid[:, :, None, :], (N_COL_BLOCKS, WIN_W, kh, KEY_BLOCK_W))
                       .reshape(N_COL_BLOCKS, WIN_W, n_loc))
    dc_idx = jnp.asarray(dc[:, :, None, :])
    qg = q.reshape(b, rows, N_COL_BLOCKS, WIN_W, NA_HEADS, NA_HEAD_DIM)
    kg = k.reshape(b, rows, GRID_W, NA_HEADS, NA_HEAD_DIM)
    vg = v.reshape(b, rows, GRID_W, NA_HEADS, NA_HEAD_DIM)
    scale = NA_HEAD_DIM ** -0.5

    def gather_blocks(band):
        blk = jnp.stack([band[:, :, s:s + KEY_BLOCK_W] for s in starts], axis=1)
        return blk.reshape(b, N_COL_BLOCKS, n_loc, NA_HEADS, NA_HEAD_DIM)

    def row_fn(r):
        r0 = jnp.clip(r - kh // 2, 0, rows - kh)
        q_row = lax.dynamic_index_in_dim(qg, r, axis=1, keepdims=False)
        k_blk = gather_blocks(lax.dynamic_slice_in_dim(kg, r0, kh, axis=1))
        v_blk = gather_blocks(lax.dynamic_slice_in_dim(vg, r0, kh, axis=1))
        dr = (r0 + jnp.arange(kh) - r + WIN_H - 1)[None, None, :, None]
        bias = rpb[:, dr, dc_idx].reshape(NA_HEADS, N_COL_BLOCKS, WIN_W, n_loc).astype(jnp.float32)
        s_loc = jnp.einsum('bjqhd,bjkhd->bhjqk', q_row, k_blk,
                           preferred_element_type=jnp.float32) * scale + bias
        s_loc = jnp.where(mask, s_loc, NEG_INF)
        s_ctx = jnp.einsum('bjqhd,bkhd->bhjqk', q_row, kc,
                           preferred_element_type=jnp.float32) * scale
        p = jax.nn.softmax(jnp.concatenate([s_loc, s_ctx], axis=-1), axis=-1).astype(v.dtype)
        return (jnp.einsum('bhjqk,bjkhd->bjqhd', p[..., :n_loc], v_blk)
                + jnp.einsum('bhjqk,bkhd->bjqhd', p[..., n_loc:], vc))

    out = lax.map(row_fn, jnp.arange(rows))
    return jnp.moveaxis(out, 0, 1).reshape(b, rows * GRID_W, NA_WIDTH)


def _ctx_attention(qc, kc, vc):
    s = jnp.einsum('bqhd,bkhd->bhqk', qc, kc, preferred_element_type=jnp.float32) * NA_HEAD_DIM ** -0.5
    p = jax.nn.softmax(s, axis=-1).astype(vc.dtype)
    o = jnp.einsum('bhqk,bkhd->bqhd', p, vc)
    return o.reshape(qc.shape[0], qc.shape[1], NA_WIDTH)


def _s5_discretize(lam_re, lam_im, log_step, b_re, b_im):
    f32 = jnp.float32
    lam_re, lam_im = lam_re.astype(f32), lam_im.astype(f32)
    b_re, b_im = b_re.astype(f32), b_im.astype(f32)
    step = jnp.exp(log_step.astype(f32))[:, None]
    mag = jnp.exp(lam_re * step)
    ang = lam_im * step
    a_re, a_im = mag * jnp.cos(ang), mag * jnp.sin(ang)
    inv = 1.0 / (lam_re * lam_re + lam_im * lam_im)
    f_re = ((a_re - 1.0) * lam_re + a_im * lam_im) * inv
    f_im = (a_im * lam_re - (a_re - 1.0) * lam_im) * inv
    bb_re = f_re[..., None] * b_re - f_im[..., None] * b_im
    bb_im = f_re[..., None] * b_im + f_im[..., None] * b_re
    return a_re, a_im, bb_re, bb_im


def _complex_scan(a_re, a_im, x_re, x_im, reverse):
    A_re = jnp.broadcast_to(a_re, x_re.shape)
    A_im = jnp.broadcast_to(a_im, x_im.shape)

    def combine(e1, e2):
        ar1, ai1, xr1, xi1 = e1
        ar2, ai2, xr2, xi2 = e2
        return (ar2 * ar1 - ai2 * ai1, ar2 * ai1 + ai2 * ar1,
                ar2 * xr1 - ai2 * xi1 + xr2, ar2 * xi1 + ai2 * xr1 + xi2)

    _, _, h_re, h_im = lax.associative_scan(combine, (A_re, A_im, x_re, x_im), reverse=reverse, axis=1)
    return h_re, h_im


def _s5_output(y, glu_w, glu_b, dtype):
    y = jax.nn.gelu(y.reshape(y.shape[0], y.shape[1], SSM_CH)).astype(dtype)
    return y * jax.nn.sigmoid(y @ glu_w + glu_b)


def _s5_mixer(u, uc, lam_re, lam_im, log_step, b_re, b_im, c_re, c_im, d_skip, glu_w, glu_b, with_ctx_out):
    f32 = jnp.float32
    b, n_lat, _ = u.shape
    n_ctx = uc.shape[1]
    ug = u.astype(f32).reshape(b, n_lat, SSM_GROUPS, SSM_GROUP)
    ucg = uc.astype(f32).reshape(b, n_ctx, SSM_GROUPS, SSM_GROUP)
    d_g = d_skip.astype(f32).reshape(SSM_GROUPS, SSM_GROUP)
    y = ug * d_g
    yc = ucg * d_g if with_ctx_out else None
    for direction in range(N_DIR):
        rev = direction == 1
        a_re, a_im, bb_re, bb_im = _s5_discretize(lam_re[direction], lam_im[direction], log_step[direction],
                                                  b_re[direction], b_im[direction])
        cr = c_re[direction].astype(f32)
        ci = c_im[direction].astype(f32)
        hc_re, hc_im = _complex_scan(a_re, a_im,
                                     jnp.einsum('bsgp,gnp->bsgn', ucg, bb_re),
                                     jnp.einsum('bsgp,gnp->bsgn', ucg, bb_im), rev)
        ctx_end = 0 if rev else -1
        lat_first = -1 if rev else 0
        h0_re, h0_im = hc_re[:, ctx_end], hc_im[:, ctx_end]
        x_re = jnp.einsum('bsgp,gnp->bsgn', ug, bb_re).at[:, lat_first].add(a_re * h0_re - a_im * h0_im)
        x_im = jnp.einsum('bsgp,gnp->bsgn', ug, bb_im).at[:, lat_first].add(a_re * h0_im + a_im * h0_re)
        h_re, h_im = _complex_scan(a_re, a_im, x_re, x_im, rev)
        y = y + jnp.einsum('bsgn,gpn->bsgp', h_re, cr) - jnp.einsum('bsgn,gpn->bsgp', h_im, ci)
        if with_ctx_out:
            yc = yc + jnp.einsum('bsgn,gpn->bsgp', hc_re, cr) - jnp.einsum('bsgn,gpn->bsgp', hc_im, ci)
    out = _s5_output(y, glu_w, glu_b, u.dtype)
    out_c = _s5_output(yc, glu_w, glu_b, u.dtype) if with_ctx_out else None
    return out, out_c


def _layer(x, xc, c, c_ctx, w_mod, b_mod, g_norm1, g_norm2, w_in, conv_w, conv_b, conv_ln_g, conv_ln_b,
           conv_pw_w, conv_pw_b, q_norm_g, k_norm_g, rpb, lam_re, lam_im, log_step, b_re, b_im, c_re, c_im,
           ssm_d, glu_w, glu_b, w_out, w_ff1, w_ff3, w_ff2, rows, with_ctx_out):
    mod = (jax.nn.silu(c) @ w_mod + b_mod)[:, None, :]
    mod_c = (jax.nn.silu(c_ctx) @ w_mod + b_mod)[None, None, :]
    sh1, sc1, g1, sh2, sc2, g2 = jnp.split(mod, 6, axis=-1)
    csh1, csc1, cg1, csh2, csc2, cg2 = jnp.split(mod_c, 6, axis=-1)
    o1 = 2 * CONV_CH
    o2 = o1 + NA_WIDTH
    o3 = o2 + NA_WIDTH
    o4 = o3 + NA_WIDTH
    conv_p = (conv_w, conv_b, conv_ln_g, conv_ln_b, conv_pw_w, conv_pw_b)

    z = _modulate(_rms_norm(x, g_norm1), sh1, sc1) @ w_in
    zc = _modulate(_rms_norm(xc, g_norm1), csh1, csc1) @ w_in

    y_conv = _conv_module(z[..., :o1], *conv_p)
    q = _rms_norm(_split_heads(z[..., o1:o2]), q_norm_g)
    k = _rms_norm(_split_heads(z[..., o2:o3]), k_norm_g)
    v = _split_heads(z[..., o3:o4])
    kc = _rms_norm(_split_heads(zc[..., o2:o3]), k_norm_g)
    vc = _split_heads(zc[..., o3:o4])
    y_na = _na_latent(q, k, v, kc, vc, rpb, rows)
    y_s5, yc_s5 = _s5_mixer(z[..., o4:], zc[..., o4:], lam_re, lam_im, log_step, b_re, b_im, c_re, c_im,
                            ssm_d, glu_w, glu_b, with_ctx_out)
    x = x + g1 * (jnp.concatenate([y_conv, y_na, y_s5], axis=-1) @ w_out)
    x = x + g2 * _swiglu(_modulate(_rms_norm(x, g_norm2), sh2, sc2), w_ff1, w_ff3, w_ff2)

    if with_ctx_out:
        yc_conv = _conv_module(zc[..., :o1], *conv_p)
        qc = _rms_norm(_split_heads(zc[..., o1:o2]), q_norm_g)
        yc_na = _ctx_attention(qc, kc, vc)
        xc = xc + cg1 * (jnp.concatenate([yc_conv, yc_na, yc_s5], axis=-1) @ w_out)
        xc = xc + cg2 * _swiglu(_modulate(_rms_norm(xc, g_norm2), csh2, csc2), w_ff1, w_ff3, w_ff2)
    return x, xc


def setup_inputs(seed: int = 0) -> dict:
    key = jax.random.key(seed)
    ks = jax.random.split(key, 40)
    f32 = jnp.float32
    nrm = lambda k, shape, s: jax.random.normal(k, shape, f32) * s
    L = DEPTH
    lam_im_base = math.pi * jnp.arange(SSM_STATE, dtype=f32)
    return {
        'x': nrm(ks[0], (BATCH, SEQ, D_MODEL), 1.0),
        'c': nrm(ks[1], (BATCH, D_MODEL), 1.0),
        'ctx': nrm(ks[2], (BATCH, CTX_LEN, D_MODEL), 1.0),
        'c_ctx': nrm(ks[3], (D_MODEL,), 1.0),
        'w_mod': nrm(ks[4], (L, D_MODEL, 6 * D_MODEL), 0.5 * D_MODEL ** -0.5),
        'b_mod': nrm(ks[5], (L, 6 * D_MODEL), 0.01),
        'g_norm1': 1.0 + nrm(ks[6], (L, D_MODEL), 0.02),
        'g_norm2': 1.0 + nrm(ks[7], (L, D_MODEL), 0.02),
        'w_in': nrm(ks[8], (L, D_MODEL, IN_WIDTH), D_MODEL ** -0.5),
        'conv_w': nrm(ks[9], (L, CONV_K, CONV_CH), CONV_K ** -0.5),
        'conv_b': nrm(ks[10], (L, CONV_CH), 0.01),
        'conv_ln_g': 1.0 + nrm(ks[11], (L, CONV_CH), 0.02),
        'conv_ln_b': nrm(ks[12], (L, CONV_CH), 0.01),
        'conv_pw_w': nrm(ks[13], (L, CONV_CH, CONV_CH), CONV_CH ** -0.5),
        'conv_pw_b': nrm(ks[14], (L, CONV_CH), 0.01),
        'q_norm_g': 1.0 + nrm(ks[15], (L, NA_HEAD_DIM), 0.02),
        'k_norm_g': 1.0 + nrm(ks[16], (L, NA_HEAD_DIM), 0.02),
        'rpb': nrm(ks[17], (L, NA_HEADS, 2 * WIN_H - 1, 2 * WIN_W - 1), 0.02),
        'ssm_lambda_re': -0.5 + nrm(ks[18], (L, N_DIR, SSM_GROUPS, SSM_STATE), 0.01),
        'ssm_lambda_im': lam_im_base + nrm(ks[19], (L, N_DIR, SSM_GROUPS, SSM_STATE), 0.01),
        'ssm_log_step': jax.random.uniform(ks[20], (L, N_DIR, SSM_GROUPS), f32,
                                           minval=math.log(1e-3), maxval=math.log(1e-1)),
        'ssm_b_re': nrm(ks[21], (L, N_DIR, SSM_GROUPS, SSM_STATE, SSM_GROUP), (2 * SSM_GROUP) ** -0.5),
        'ssm_b_im': nrm(ks[22], (L, N_DIR, SSM_GROUPS, SSM_STATE, SSM_GROUP), (2 * SSM_GROUP) ** -0.5),
        'ssm_c_re': nrm(ks[23], (L, N_DIR, SSM_GROUPS, SSM_GROUP, SSM_STATE), (2 * SSM_STATE) ** -0.5),
        'ssm_c_im': nrm(ks[24], (L, N_DIR, SSM_GROUPS, SSM_GROUP, SSM_STATE), (2 * SSM_STATE) ** -0.5),
        'ssm_d': nrm(ks[25], (L, SSM_CH), 1.0),
        'ssm_glu_w': nrm(ks[26], (L, SSM_CH, SSM_CH), SSM_CH ** -0.5),
        'ssm_glu_b': nrm(ks[27], (L, SSM_CH), 0.01),
        'w_out': nrm(ks[28], (L, MIX_WIDTH, D_MODEL), MIX_WIDTH ** -0.5),
        'w_ff1': nrm(ks[29], (L, D_MODEL, D_FF), D_MODEL ** -0.5),
        'w_ff3': nrm(ks[30], (L, D_MODEL, D_FF), D_MODEL ** -0.5),
        'w_ff2': nrm(ks[31], (L, D_FF, D_MODEL), D_FF ** -0.5),
    }


def reference(x, c, ctx, c_ctx, w_mod, b_mod, g_norm1, g_norm2, w_in, conv_w, conv_b, conv_ln_g, conv_ln_b,
              conv_pw_w, conv_pw_b, q_norm_g, k_norm_g, rpb, ssm_lambda_re, ssm_lambda_im, ssm_log_step,
              ssm_b_re, ssm_b_im, ssm_c_re, ssm_c_im, ssm_d, ssm_glu_w, ssm_glu_b, w_out, w_ff1, w_ff3, w_ff2):
    rows = x.shape[1] // GRID_W
    x_lat, x_ctx = x, ctx
    for layer in range(DEPTH):
        x_lat, x_ctx = _layer(
            x_lat, x_ctx, c, c_ctx, w_mod[layer], b_mod[layer], g_norm1[layer], g_norm2[layer], w_in[layer],
            conv_w[layer], conv_b[layer], conv_ln_g[layer], conv_ln_b[layer], conv_pw_w[layer], conv_pw_b[layer],
            q_norm_g[layer], k_norm_g[layer], rpb[layer], ssm_lambda_re[layer], ssm_lambda_im[layer],
            ssm_log_step[layer], ssm_b_re[layer], ssm_b_im[layer], ssm_c_re[layer], ssm_c_im[layer],
            ssm_d[layer], ssm_glu_w[layer], ssm_glu_b[layer], w_out[layer], w_ff1[layer], w_ff3[layer],
            w_ff2[layer], rows, layer < DEPTH - 1)
    return x_lat
```

```python
import functools

import jax
import jax.numpy as jnp
from jax import lax
from jax.experimental import pallas as pl
from jax.experimental.pallas import tpu as pltpu

F32 = jnp.float32
BF16 = jnp.bfloat16

GRID_W = 64
WIN_H = 8
WIN_W = 16
HEAD_DIM = 128
CONV_K = 31
SSM_GROUP = 16
SSM_STATE = 64
EPS = 1e-6
NEG_INF = -1e30

LANES = 128
SUBLANES = 8
VMEM_LIMIT = 56 * 1024 * 1024

SEQ_TILE = 256
SCAN_PITCH = SEQ_TILE + SUBLANES
HALO = 16


def _cparams(*sem):
    return pltpu.CompilerParams(dimension_semantics=sem, vmem_limit_bytes=VMEM_LIMIT)


def _sigmoid(x):
    return 1.0 / (1.0 + jnp.exp(-x))


def _silu(x):
    return x * _sigmoid(x)


def _mod_kernel(c_ref, w_ref, b_ref, o_ref):
    s = _silu(c_ref[...]).astype(BF16)
    o_ref[...] = jnp.dot(s, w_ref[...].astype(BF16), preferred_element_type=F32) + b_ref[...]


def _modulation(cc, w_mod, layer, b_mod):
    _, d, n = w_mod.shape
    tn = 512
    return pl.pallas_call(
        _mod_kernel,
        grid=(n // tn,),
        in_specs=[
            pl.BlockSpec((SUBLANES, d), lambda j: (0, 0)),
            pl.BlockSpec((None, d, tn), lambda j: (layer, 0, j)),
            pl.BlockSpec((1, tn), lambda j: (0, j)),
        ],
        out_specs=pl.BlockSpec((SUBLANES, tn), lambda j: (0, j)),
        out_shape=jax.ShapeDtypeStruct((SUBLANES, n), F32),
        compiler_params=_cparams("parallel"),
        name="adaln_mod",
    )(cc, w_mod, b_mod.reshape(1, n))


def _norm_mod_kernel(x_ref, g_ref, sh_ref, sc_ref, o_ref, *, n_ctx_tiles):
    i = pl.program_id(0)
    x = x_ref[...]
    y = x * lax.rsqrt(jnp.mean(x * x, axis=-1, keepdims=True) + EPS) * g_ref[...]
    row = jnp.where(i < n_ctx_tiles, 1, 0)
    sh = sh_ref[pl.ds(row, 1), :]
    sc = sc_ref[pl.ds(row, 1), :]
    o_ref[...] = (y * (1.0 + sc) + sh).astype(BF16)


def _norm_mod(x, g, sh, sc, n_ctx):
    m, d = x.shape
    tm = SEQ_TILE
    return pl.pallas_call(
        functools.partial(_norm_mod_kernel, n_ctx_tiles=n_ctx // tm),
        grid=(m // tm,),
        in_specs=[
            pl.BlockSpec((tm, d), lambda i: (i, 0)),
            pl.BlockSpec((1, d), lambda i: (0, 0)),
            pl.BlockSpec((2, d), lambda i: (0, 0)),
            pl.BlockSpec((2, d), lambda i: (0, 0)),
        ],
        out_specs=pl.BlockSpec((tm, d), lambda i: (i, 0)),
        out_shape=jax.ShapeDtypeStruct((m, d), BF16),
        compiler_params=_cparams("parallel"),
        name="norm_mod",
    )(x, g.reshape(1, d), sh, sc)


def _proj_kernel(a_ref, w_ref, o_ref):
    o_ref[...] = jnp.dot(a_ref[...], w_ref[...].astype(BF16),
                         preferred_element_type=F32).astype(o_ref.dtype)


def _proj(a, w, layer, out_dtype, tm, tn):
    m, k = a.shape
    n = w.shape[2]
    return pl.pallas_call(
        _proj_kernel,
        grid=(m // tm, n // tn),
        in_specs=[
            pl.BlockSpec((tm, k), lambda i, j: (i, 0)),
            pl.BlockSpec((None, k, tn), lambda i, j: (layer, 0, j)),
        ],
        out_specs=pl.BlockSpec((tm, tn), lambda i, j: (i, j)),
        out_shape=jax.ShapeDtypeStruct((m, n), out_dtype),
        compiler_params=_cparams("parallel", "arbitrary"),
        name="proj",
    )(a, w)


def _gate_rows(gate_ref, tm, n_ctx):
    rows = pl.program_id(0) * tm + lax.broadcasted_iota(jnp.int32, (tm, 1), 0)
    return jnp.where(rows < n_ctx, gate_ref[1:2, :], gate_ref[0:1, :])


def _mix_out_kernel(a0_ref, a1_ref, a2_ref, w_ref, res_ref, gate_ref, o_ref, *, tm, n_ctx):
    k0 = a0_ref.shape[1]
    k1 = a1_ref.shape[1]
    k2 = a2_ref.shape[1]
    acc = jnp.dot(a0_ref[...], w_ref[0:k0, :].astype(BF16), preferred_element_type=F32)
    acc += jnp.dot(a1_ref[...], w_ref[k0:k0 + k1, :].astype(BF16), preferred_element_type=F32)
    acc += jnp.dot(a2_ref[...], w_ref[k0 + k1:k0 + k1 + k2, :].astype(BF16),
                   preferred_element_type=F32)
    o_ref[...] = res_ref[...] + _gate_rows(gate_ref, tm, n_ctx) * acc


def _mix_out(a0, a1, a2, w, layer, res, gate, n_ctx, tm, tn):
    m = a0.shape[0]
    _, k, n = w.shape
    return pl.pallas_call(
        functools.partial(_mix_out_kernel, tm=tm, n_ctx=n_ctx),
        grid=(m // tm, n // tn),
        in_specs=[
            pl.BlockSpec((tm, a0.shape[1]), lambda i, j: (i, 0)),
            pl.BlockSpec((tm, a1.shape[1]), lambda i, j: (i, 0)),
            pl.BlockSpec((tm, a2.shape[1]), lambda i, j: (i, 0)),
            pl.BlockSpec((None, k, tn), lambda i, j: (layer, 0, j)),
            pl.BlockSpec((tm, tn), lambda i, j: (i, j)),
            pl.BlockSpec((2, tn), lambda i, j: (0, j)),
        ],
        out_specs=pl.BlockSpec((tm, tn), lambda i, j: (i, j)),
        out_shape=jax.ShapeDtypeStruct((m, n), F32),
        compiler_params=_cparams("parallel", "arbitrary"),
        name="mix_out",
    )(a0, a1, a2, w, res, gate)


def _ffn_up_kernel(a_ref, w1_ref, w3_ref, o_ref):
    a = a_ref[...]
    t = jnp.dot(a, w1_ref[...].astype(BF16), preferred_element_type=F32)
    u = jnp.dot(a, w3_ref[...].astype(BF16), preferred_element_type=F32)
    o_ref[...] = (_silu(t) * u).astype(o_ref.dtype)


def _ffn_up(a, w1, w3, layer, tm, tn):
    m, k = a.shape
    n = w1.shape[2]
    return pl.pallas_call(
        _ffn_up_kernel,
        grid=(m // tm, n // tn),
        in_specs=[
            pl.BlockSpec((tm, k), lambda i, j: (i, 0)),
            pl.BlockSpec((None, k, tn), lambda i, j: (layer, 0, j)),
            pl.BlockSpec((None, k, tn), lambda i, j: (layer, 0, j)),
        ],
        out_specs=pl.BlockSpec((tm, tn), lambda i, j: (i, j)),
        out_shape=jax.ShapeDtypeStruct((m, n), BF16),
        compiler_params=_cparams("parallel", "arbitrary"),
        name="ffn_up",
    )(a, w1, w3)


def _ffn_down_kernel(a_ref, w_ref, res_ref, gate_ref, o_ref, *, tm, n_ctx):
    acc = jnp.dot(a_ref[...], w_ref[...].astype(BF16), preferred_element_type=F32)
    o_ref[...] = res_ref[...] + _gate_rows(gate_ref, tm, n_ctx) * acc


def _ffn_down(a, w, layer, res, gate, n_ctx, tm, tn):
    m, k = a.shape
    n = w.shape[2]
    return pl.pallas_call(
        functools.partial(_ffn_down_kernel, tm=tm, n_ctx=n_ctx),
        grid=(m // tm, n // tn),
        in_specs=[
            pl.BlockSpec((tm, k), lambda i, j: (i, 0), pipeline_mode=pl.Buffered(1)),
            pl.BlockSpec((None, k, tn), lambda i, j: (layer, 0, j)),
            pl.BlockSpec((tm, tn), lambda i, j: (i, j)),
            pl.BlockSpec((2, tn), lambda i, j: (0, j)),
        ],
        out_specs=pl.BlockSpec((tm, tn), lambda i, j: (i, j)),
        out_shape=jax.ShapeDtypeStruct((m, n), F32),
        compiler_params=_cparams("parallel", "arbitrary"),
        name="ffn_down",
    )(a, w, res, gate)


def _conv_kernel(main_ref, prev_ref, next_ref, cw_ref, cb_ref, lng_ref, lnb_ref, pw_ref, pwb_ref,
                 o_ref, uext_ref, yc_ref, wbf_ref, *, n_ctx_tiles, n_tiles):
    i = pl.program_id(0)
    tl = main_ref.shape[0]
    ch = o_ref.shape[1]
    nblk = ch // LANES

    @pl.when(i == 0)
    def _():
        wbf_ref[...] = pw_ref[...].astype(BF16)

    def glu(a):
        return a[:, :ch] * _sigmoid(a[:, ch:])

    is_start = jnp.logical_or(i == 0, i == n_ctx_tiles)
    is_end = jnp.logical_or(i == n_ctx_tiles - 1, i == n_tiles - 1)
    u = glu(main_ref[...])
    up = jnp.where(is_start, 0.0, glu(prev_ref[...]))
    un = jnp.where(is_end, 0.0, glu(next_ref[...]))
    for c in range(nblk):
        sl = slice(c * LANES, (c + 1) * LANES)
        uext_ref[c, 0:HALO, :] = up[:, sl]
        uext_ref[c, HALO:HALO + tl, :] = u[:, sl]
        uext_ref[c, HALO + tl:2 * HALO + tl, :] = un[:, sl]

    def chan_block(c, carry):
        acc = jnp.zeros((tl, LANES), F32)
        for k in range(CONV_K):
            acc = acc + uext_ref[c, pl.ds(k + HALO - CONV_K // 2, tl), :] * cw_ref[c, k:k + 1, :]
        yc_ref[c] = acc + cb_ref[c]
        return carry

    lax.fori_loop(0, nblk, chan_block, 0)

    y = jnp.concatenate([yc_ref[c] for c in range(nblk)], axis=1)
    mu = jnp.mean(y, axis=-1, keepdims=True)
    yd = y - mu
    var = jnp.mean(yd * yd, axis=-1, keepdims=True)
    yn = yd * lax.rsqrt(var + EPS) * lng_ref[...] + lnb_ref[...]
    s = _silu(yn).astype(BF16)
    o_ref[...] = (jnp.dot(s, wbf_ref[...], preferred_element_type=F32) + pwb_ref[...]).astype(o_ref.dtype)


def _conv_module(z, conv_w, conv_b, ln_g, ln_b, pw_w, layer, pw_b, n_ctx):
    m = z.shape[0]
    ch = conv_w.shape[1]
    tl = SEQ_TILE
    nblk = ch // LANES
    n_tiles = m // tl
    hb = tl // HALO
    cw = conv_w.reshape(CONV_K, nblk, LANES).transpose(1, 0, 2)
    cb = conv_b.reshape(nblk, 1, LANES)
    return pl.pallas_call(
        functools.partial(_conv_kernel, n_ctx_tiles=n_ctx // tl, n_tiles=n_tiles),
        grid=(n_tiles,),
        in_specs=[
            pl.BlockSpec((tl, 2 * ch), lambda i: (i, 0)),
            pl.BlockSpec((HALO, 2 * ch), lambda i: (jnp.maximum(i * hb - 1, 0), 0)),
            pl.BlockSpec((HALO, 2 * ch), lambda i: (jnp.minimum((i + 1) * hb, n_tiles * hb - 1), 0)),
            pl.BlockSpec((nblk, CONV_K, LANES), lambda i: (0, 0, 0)),
            pl.BlockSpec((nblk, 1, LANES), lambda i: (0, 0, 0)),
            pl.BlockSpec((1, ch), lambda i: (0, 0)),
            pl.BlockSpec((1, ch), lambda i: (0, 0)),
            pl.BlockSpec((None, ch, ch), lambda i: (layer, 0, 0)),
            pl.BlockSpec((1, ch), lambda i: (0, 0)),
        ],
        out_specs=pl.BlockSpec((tl, ch), lambda i: (i, 0)),
        out_shape=jax.ShapeDtypeStruct((m, ch), BF16),
        scratch_shapes=[
            pltpu.VMEM((nblk, tl + 2 * HALO, LANES), F32),
            pltpu.VMEM((nblk, tl, LANES), F32),
            pltpu.VMEM((ch, ch), BF16),
        ],
        compiler_params=_cparams("arbitrary"),
        name="conv_module",
    )(z, z, z, cw, cb, ln_g.reshape(1, ch), ln_b.reshape(1, ch), pw_w, pw_b.reshape(1, ch))


def _bias_table_kernel(rpb_ref, o_ref):
    h = pl.program_id(0)
    qc = lax.broadcasted_iota(jnp.int32, (GRID_W, GRID_W), 0)
    kc = lax.broadcasted_iota(jnp.int32, (GRID_W, GRID_W), 1)
    qstart = jnp.clip(qc - WIN_W // 2, 0, GRID_W - WIN_W)
    valid = jnp.logical_and(kc >= qstart, kc < qstart + WIN_W)
    dc = jnp.where(valid, kc - qc + WIN_W - 1, -1)
    tabs = []
    for dr in range(2 * WIN_H - 1):
        t = jnp.full((GRID_W, GRID_W), NEG_INF, F32)
        for d in range(2 * WIN_W - 1):
            t = jnp.where(dc == d, rpb_ref[h, dr, d], t)
        tabs.append(t)
    for off in range(WIN_H):
        o_ref[0, off] = jnp.concatenate([tabs[off + i] for i in range(WIN_H)], axis=1)


def _bias_table(rpb):
    heads = rpb.shape[0]
    return pl.pallas_call(
        _bias_table_kernel,
        grid=(heads,),
        in_specs=[pl.BlockSpec(memory_space=pltpu.SMEM)],
        out_specs=pl.BlockSpec((1, WIN_H, GRID_W, WIN_H * GRID_W), lambda h: (h, 0, 0, 0)),
        out_shape=jax.ShapeDtypeStruct((heads, WIN_H, GRID_W, WIN_H * GRID_W), F32),
        compiler_params=_cparams("parallel"),
        name="bias_table",
    )(rpb)


def _attn_kernel(q_ref, k_ref, v_ref, qg_ref, kg_ref, bias_ref, o_ref, kn_ref, vb_ref, *, n_ctx, rows):
    m = q_ref.shape[0]
    scale = HEAD_DIM ** -0.5
    band = WIN_H * GRID_W
    nt = (((1,), (1,)), ((), ()))

    def rms(x, g):
        return x * lax.rsqrt(jnp.mean(x * x, axis=-1, keepdims=True) + EPS) * g

    def prep(c, carry):
        r = pl.multiple_of(c * SEQ_TILE, SEQ_TILE)
        kn_ref[pl.ds(r, SEQ_TILE), :] = rms(k_ref[pl.ds(r, SEQ_TILE), :], kg_ref[...]).astype(BF16)
        vb_ref[pl.ds(r, SEQ_TILE), :] = v_ref[pl.ds(r, SEQ_TILE), :].astype(BF16)
        return carry

    lax.fori_loop(0, m // SEQ_TILE, prep, 0)

    kc = kn_ref[0:n_ctx, :]
    vc = vb_ref[0:n_ctx, :]

    qc = rms(q_ref[0:n_ctx, :], qg_ref[...]).astype(BF16)
    s = lax.dot_general(qc, kc, nt, preferred_element_type=F32) * scale
    p = jnp.exp(s - jnp.max(s, axis=-1, keepdims=True))
    l = jnp.sum(p, axis=-1, keepdims=True)
    o_ref[0:n_ctx, :] = (jnp.dot(p.astype(BF16), vc, preferred_element_type=F32) / l).astype(o_ref.dtype)

    def row(r, carry):
        r0 = jnp.clip(r - WIN_H // 2, 0, rows - WIN_H)
        off = r0 - r + WIN_H - 1
        qs = pl.multiple_of(n_ctx + r * GRID_W, GRID_W)
        ks = pl.multiple_of(n_ctx + r0 * GRID_W, GRID_W)
        q = rms(q_ref[pl.ds(qs, GRID_W), :], qg_ref[...]).astype(BF16)
        kb = kn_ref[pl.ds(ks, band), :]
        s_loc = lax.dot_general(q, kb, nt, preferred_element_type=F32) * scale + bias_ref[0, off]
        s_ctx = lax.dot_general(q, kc, nt, preferred_element_type=F32) * scale
        mx = jnp.maximum(jnp.max(s_loc, axis=-1, keepdims=True), jnp.max(s_ctx, axis=-1, keepdims=True))
        p_loc = jnp.exp(s_loc - mx)
        p_ctx = jnp.exp(s_ctx - mx)
        den = jnp.sum(p_loc, axis=-1, keepdims=True) + jnp.sum(p_ctx, axis=-1, keepdims=True)
        o = jnp.dot(p_loc.astype(BF16), vb_ref[pl.ds(ks, band), :], preferred_element_type=F32)
        o = o + jnp.dot(p_ctx.astype(BF16), vc, preferred_element_type=F32)
        o_ref[pl.ds(qs, GRID_W), :] = (o / den).astype(o_ref.dtype)
        return carry

    lax.fori_loop(0, rows, row, 0)


def _attention(z, q_g, k_g, bias_tab, n_ctx, col0, width):
    m = z.shape[0]
    heads = width // HEAD_DIM
    rows = (m - n_ctx) // GRID_W
    cb = col0 // HEAD_DIM
    return pl.pallas_call(
        functools.partial(_attn_kernel, n_ctx=n_ctx, rows=rows),
        grid=(heads,),
        in_specs=[
            pl.BlockSpec((m, HEAD_DIM), lambda h: (0, cb + h)),
            pl.BlockSpec((m, HEAD_DIM), lambda h: (0, cb + heads + h)),
            pl.BlockSpec((m, HEAD_DIM), lambda h: (0, cb + 2 * heads + h)),
            pl.BlockSpec((1, HEAD_DIM), lambda h: (0, 0)),
            pl.BlockSpec((1, HEAD_DIM), lambda h: (0, 0)),
            pl.BlockSpec((1, WIN_H, GRID_W, WIN_H * GRID_W), lambda h: (h, 0, 0, 0)),
        ],
        out_specs=pl.BlockSpec((m, HEAD_DIM), lambda h: (0, h)),
        out_shape=jax.ShapeDtypeStruct((m, width), BF16),
        scratch_shapes=[pltpu.VMEM((m, HEAD_DIM), BF16), pltpu.VMEM((m, HEAD_DIM), BF16)],
        compiler_params=_cparams("parallel"),
        name="nbr_attention",
    )(z, z, z, q_g.reshape(1, HEAD_DIM), k_g.reshape(1, HEAD_DIM), bias_tab)


def _discretize_kernel(lr_ref, li_ref, ls_ref, br_ref, bi_ref, ar_ref, ai_ref, bbr_ref, bbi_ref):
    lr = lr_ref[0]
    li = li_ref[0]
    step = jnp.exp(ls_ref[0])
    mag = jnp.exp(lr * step)
    ang = li * step
    a_re = mag * jnp.cos(ang)
    a_im = mag * jnp.sin(ang)
    inv = 1.0 / (lr * lr + li * li)
    f_re = ((a_re - 1.0) * lr + a_im * li) * inv
    f_im = (a_im * lr - (a_re - 1.0) * li) * inv
    br = br_ref[0]
    bi = bi_ref[0]
    ar_ref[0] = a_re
    ai_ref[0] = a_im
    bbr_ref[0] = f_re * br - f_im * bi
    bbi_ref[0] = f_re * bi + f_im * br


def _discretize(lam_re, lam_im, log_step, b_re, b_im):
    n_dir, g, n = lam_re.shape
    p = b_re.shape[-1]
    rep = lambda a: jnp.repeat(a, p, axis=1)
    lr = rep(lam_re)
    li = rep(lam_im)
    ls = rep(jnp.broadcast_to(log_step[:, :, None], (n_dir, g, n)))
    br = b_re.transpose(0, 1, 3, 2).reshape(n_dir, g * p, n)
    bi = b_im.transpose(0, 1, 3, 2).reshape(n_dir, g * p, n)
    spec = pl.BlockSpec((1, g * p, n), lambda d: (d, 0, 0))
    shp = jax.ShapeDtypeStruct((n_dir, g * p, n), F32)
    return pl.pallas_call(
        _discretize_kernel,
        grid=(n_dir,),
        in_specs=[spec] * 5,
        out_specs=[spec] * 4,
        out_shape=[shp] * 4,
        compiler_params=_cparams("parallel"),
        name="s5_discretize",
    )(lr, li, ls, br, bi)


def _scan_weights(a_re, a_im, bbt_re, bbt_im, c_re, c_im):
    g, n = a_re.shape[0] // SSM_GROUP, a_re.shape[1]
    gb = LANES // SSM_GROUP
    nb = g // gb
    eye = jnp.eye(gb, dtype=F32)
    ar = a_re[::SSM_GROUP].reshape(g * n // LANES, LANES)
    ai = a_im[::SSM_GROUP].reshape(g * n // LANES, LANES)

    def in_mat(bbt):
        blk = bbt.reshape(nb, gb, SSM_GROUP, n)
        return jnp.einsum("bgpn,gh->bgphn", blk, eye).reshape(nb, LANES, gb * n)

    def out_mat(c):
        blk = c.reshape(nb, gb, SSM_GROUP, n)
        return jnp.einsum("bgpn,gh->bgnhp", blk, eye).reshape(nb, gb * n, LANES)

    wb = jnp.concatenate([in_mat(bbt_re), in_mat(bbt_im)], axis=2).astype(BF16)
    wc = jnp.concatenate([out_mat(c_re), -out_mat(c_im)], axis=1).astype(BF16)
    return ar, ai, wb, wc


def _scan_chunk(u_ref, wb_ref, ar_ref, ai_ref, wc_ref, xs_ref, hs_ref, reverse):
    t_len = u_ref.shape[0]
    nb = wb_ref.shape[0]
    half = wb_ref.shape[2] // 2
    spb = half // LANES
    n_re = nb * spb
    u = u_ref[...].astype(BF16)
    for b in range(nb):
        xb = jnp.dot(u[:, b * LANES:(b + 1) * LANES], wb_ref[b], preferred_element_type=F32)
        for k in range(spb):
            xs_ref[pl.ds((b * spb + k) * SCAN_PITCH, t_len), :] = xb[:, k * LANES:(k + 1) * LANES]
            xs_ref[pl.ds((n_re + b * spb + k) * SCAN_PITCH, t_len), :] = (
                xb[:, half + k * LANES:half + (k + 1) * LANES])

    ar = ar_ref[...]
    ai = ai_ref[...]
    im0 = n_re * SCAN_PITCH

    def step(j, carry):
        h_re, h_im = carry
        t = (t_len - 1 - j) if reverse else j
        x_re = xs_ref[pl.ds(t, n_re, stride=SCAN_PITCH), :]
        x_im = xs_ref[pl.ds(im0 + t, n_re, stride=SCAN_PITCH), :]
        n_h_re = ar * h_re - ai * h_im + x_re
        n_h_im = ar * h_im + ai * h_re + x_im
        xs_ref[pl.ds(t, n_re, stride=SCAN_PITCH), :] = n_h_re
        xs_ref[pl.ds(im0 + t, n_re, stride=SCAN_PITCH), :] = n_h_im
        return n_h_re, n_h_im

    h_re, h_im = lax.fori_loop(0, t_len, step, (hs_ref[0], hs_ref[1]), unroll=8)
    hs_ref[0] = h_re
    hs_ref[1] = h_im

    ys = []
    for b in range(nb):
        slabs = [xs_ref[pl.ds((b * spb + k) * SCAN_PITCH, t_len), :] for k in range(spb)]
        slabs += [xs_ref[pl.ds((n_re + b * spb + k) * SCAN_PITCH, t_len), :] for k in range(spb)]
        hb = jnp.concatenate(slabs, axis=1).astype(BF16)
        ys.append(jnp.dot(hb, wc_ref[b], preferred_element_type=F32))
    return jnp.concatenate(ys, axis=1)


def _scan_fwd_kernel(u_ref, wb_ref, ar_ref, ai_ref, wc_ref, o_ref, xs_ref, hs_ref):
    @pl.when(pl.program_id(0) == 0)
    def _():
        hs_ref[...] = jnp.zeros_like(hs_ref)

    o_ref[...] = _scan_chunk(u_ref, wb_ref, ar_ref, ai_ref, wc_ref, xs_ref, hs_ref, False)


def _scan_bwd_kernel(u_ref, wb_ref, ar_ref, ai_ref, wc_ref, yf_ref, d_ref, gw_ref, gb_ref, o_ref,
                     xs_ref, hs_ref, gwbf_ref):
    @pl.when(pl.program_id(0) == 0)
    def _():
        hs_ref[...] = jnp.zeros_like(hs_ref)
        gwbf_ref[...] = gw_ref[...].astype(BF16)

    y = _scan_chunk(u_ref, wb_ref, ar_ref, ai_ref, wc_ref, xs_ref, hs_ref, True)
    y = y + yf_ref[...] + u_ref[...] * d_ref[...]
    g = jax.nn.gelu(y)
    gate = jnp.dot(g.astype(BF16), gwbf_ref[...], preferred_element_type=F32) + gb_ref[...]
    o_ref[...] = (g * _sigmoid(gate)).astype(o_ref.dtype)


def _s5_mixer(z, col0, ch, disc, c_re, c_im, d_skip, glu_w, layer, glu_b, n_ctx):
    m = z.shape[0]
    t_len = SEQ_TILE
    assert n_ctx == t_len
    n_chunks = m // t_len
    cb = col0 // ch
    a_re, a_im, bbt_re, bbt_im = disc
    w_f = _scan_weights(a_re[0], a_im[0], bbt_re[0], bbt_im[0], c_re[0], c_im[0])
    w_b = _scan_weights(a_re[1], a_im[1], bbt_re[1], bbt_im[1], c_re[1], c_im[1])
    ar, ai, wb, wc = w_f
    nb = wb.shape[0]
    n_slab = 2 * ar.shape[0]

    def const_spec(a):
        nd = a.ndim
        return pl.BlockSpec(a.shape, lambda i: (0,) * nd)

    scratch = [pltpu.VMEM((n_slab * SCAN_PITCH, LANES), F32), pltpu.VMEM((2,) + ar.shape, F32)]

    y_f = pl.pallas_call(
        _scan_fwd_kernel,
        grid=(n_chunks,),
        in_specs=[pl.BlockSpec((t_len, ch), lambda i: (i, cb))] + [const_spec(a) for a in (wb, ar, ai, wc)],
        out_specs=pl.BlockSpec((t_len, ch), lambda i: (i, 0)),
        out_shape=jax.ShapeDtypeStruct((m, ch), F32),
        scratch_shapes=scratch,
        compiler_params=_cparams("arbitrary"),
        name="s5_scan_fwd",
    )(z, wb, ar, ai, wc)

    ar, ai, wb, wc = w_b
    chunk = lambda i: jnp.where(i == 0, 0, n_chunks - i)
    return pl.pallas_call(
        _scan_bwd_kernel,
        grid=(n_chunks,),
        in_specs=[pl.BlockSpec((t_len, ch), lambda i: (chunk(i), cb))]
        + [const_spec(a) for a in (wb, ar, ai, wc)]
        + [
            pl.BlockSpec((t_len, ch), lambda i: (chunk(i), 0)),
            pl.BlockSpec((1, ch), lambda i: (0, 0)),
            pl.BlockSpec((None, ch, ch), lambda i: (layer, 0, 0)),
            pl.BlockSpec((1, ch), lambda i: (0, 0)),
        ],
        out_specs=pl.BlockSpec((t_len, ch), lambda i: (chunk(i), 0)),
        out_shape=jax.ShapeDtypeStruct((m, ch), BF16),
        scratch_shapes=scratch + [pltpu.VMEM((ch, ch), BF16)],
        compiler_params=_cparams("arbitrary"),
        name="s5_scan_bwd",
    )(z, wb, ar, ai, wc, y_f, d_skip.reshape(1, ch), glu_w, glu_b.reshape(1, ch))


def _layer(xa, cc, n_ctx, layer, w_mod, b_mod, g_norm1, g_norm2, w_in, conv_w, conv_b, conv_ln_g, conv_ln_b,
           conv_pw_w, conv_pw_b, q_norm_g, k_norm_g, rpb, lam_re, lam_im, log_step, b_re, b_im, c_re, c_im,
           ssm_d, glu_w, glu_b, w_out, w_ff1, w_ff3, w_ff2):
    d = xa.shape[1]
    conv_ch = conv_w.shape[1]
    ssm_ch = ssm_d.shape[0]
    na_width = (w_in.shape[2] - 2 * conv_ch - ssm_ch) // 3
    o1 = 2 * conv_ch
    o4 = o1 + 3 * na_width
    tm = xa.shape[0] // 8

    mod = _modulation(cc, w_mod, layer, b_mod)[:2]
    sh1, sc1, g1, sh2, sc2, g2 = [mod[:, i * d:(i + 1) * d] for i in range(6)]

    z = _proj(_norm_mod(xa, g_norm1, sh1, sc1, n_ctx), w_in, layer, F32, tm, 512)

    y_conv = _conv_module(z, conv_w, conv_b, conv_ln_g, conv_ln_b, conv_pw_w, layer, conv_pw_b, n_ctx)
    y_na = _attention(z, q_norm_g, k_norm_g, _bias_table(rpb), n_ctx, o1, na_width)
    disc = _discretize(lam_re, lam_im, log_step, b_re, b_im)
    y_s5 = _s5_mixer(z, o4, ssm_ch, disc, c_re, c_im, ssm_d, glu_w, layer, glu_b, n_ctx)

    xa = _mix_out(y_conv, y_na, y_s5, w_out, layer, xa, g1, n_ctx, tm, 512)
    hidden = _ffn_up(_norm_mod(xa, g_norm2, sh2, sc2, n_ctx), w_ff1, w_ff3, layer, tm, 256)
    return _ffn_down(hidden, w_ff2, layer, xa, g2, n_ctx, tm // 2, 256)


def kernel(x, c, ctx, c_ctx, w_mod, b_mod, g_norm1, g_norm2, w_in, conv_w, conv_b, conv_ln_g, conv_ln_b,
           conv_pw_w, conv_pw_b, q_norm_g, k_norm_g, rpb, ssm_lambda_re, ssm_lambda_im, ssm_log_step,
           ssm_b_re, ssm_b_im, ssm_c_re, ssm_c_im, ssm_d, ssm_glu_w, ssm_glu_b, w_out, w_ff1, w_ff3, w_ff2):
    batch, seq, d = x.shape
    n_ctx = ctx.shape[1]
    assert batch == 1 and n_ctx == SEQ_TILE and seq % GRID_W == 0
    depth = w_mod.shape[0]
    xa = jnp.concatenate([ctx[0], x[0]], axis=0)
    cc = jnp.zeros((SUBLANES, d), F32).at[0].set(c[0]).at[1].set(c_ctx)
    for l in range(depth):
        xa = _layer(xa, cc, n_ctx, l, w_mod, b_mod[l], g_norm1[l], g_norm2[l], w_in, conv_w[l], conv_b[l],
                    conv_ln_g[l], conv_ln_b[l], conv_pw_w, conv_pw_b[l], q_norm_g[l], k_norm_g[l], rpb[l],
                    ssm_lambda_re[l], ssm_lambda_im[l], ssm_log_step[l], ssm_b_re[l], ssm_b_im[l],
                    ssm_c_re[l], ssm_c_im[l], ssm_d[l], ssm_glu_w, ssm_glu_b[l], w_out, w_ff1,
                    w_ff3, w_ff2)
    return xa[n_ctx:][None]
```

```python
import functools

import jax
import jax.numpy as jnp
from jax import lax
from jax.experimental import pallas as pl
from jax.experimental.pallas import tpu as pltpu

F32 = jnp.float32
BF16 = jnp.bfloat16

GRID_W = 64
WIN_H = 8
WIN_W = 16
HEAD_DIM = 128
CONV_K = 31
SSM_GROUP = 16
SSM_STATE = 64
EPS = 1e-6
NEG_INF = -1e30
LOG2E = 1.4426950408889634

LANES = 128
SUBLANES = 8
VMEM_LIMIT = 56 * 1024 * 1024

SEQ_TILE = 256
SCAN_PITCH = SEQ_TILE + SUBLANES
HALO = 16


def _cparams(*sem):
    return pltpu.CompilerParams(dimension_semantics=sem, vmem_limit_bytes=VMEM_LIMIT)


def _sigmoid(x):
    return 1.0 / (1.0 + jnp.exp(-x))


def _silu(x):
    return x * _sigmoid(x)


def _mod_kernel(c_ref, w_ref, b_ref, o_ref):
    s = _silu(c_ref[...]).astype(BF16)
    o_ref[...] = jnp.dot(s, w_ref[...].astype(BF16), preferred_element_type=F32) + b_ref[...]


def _modulation(cc, w_mod, layer, b_mod):
    _, d, n = w_mod.shape
    tn = 512
    return pl.pallas_call(
        _mod_kernel,
        grid=(n // tn,),
        in_specs=[
            pl.BlockSpec((SUBLANES, d), lambda j: (0, 0)),
            pl.BlockSpec((None, d, tn), lambda j: (layer, 0, j)),
            pl.BlockSpec((1, tn), lambda j: (0, j)),
        ],
        out_specs=pl.BlockSpec((SUBLANES, tn), lambda j: (0, j)),
        out_shape=jax.ShapeDtypeStruct((SUBLANES, n), F32),
        compiler_params=_cparams("parallel"),
        name="adaln_mod",
    )(cc, w_mod, b_mod.reshape(1, n))


def _norm_mod_kernel(x_ref, g_ref, sh_ref, sc_ref, o_ref, *, n_lat_tiles):
    i = pl.program_id(0)
    x = x_ref[...]
    y = x * lax.rsqrt(jnp.mean(x * x, axis=-1, keepdims=True) + EPS) * g_ref[...]
    row = jnp.where(i >= n_lat_tiles, 1, 0)
    sh = sh_ref[pl.ds(row, 1), :]
    sc = sc_ref[pl.ds(row, 1), :]
    o_ref[...] = (y * (1.0 + sc) + sh).astype(BF16)


def _norm_mod(x, g, sh, sc, n_lat, m):
    d = x.shape[1]
    tm = SEQ_TILE
    return pl.pallas_call(
        functools.partial(_norm_mod_kernel, n_lat_tiles=n_lat // tm),
        grid=(m // tm,),
        in_specs=[
            pl.BlockSpec((tm, d), lambda i: (i, 0)),
            pl.BlockSpec((1, d), lambda i: (0, 0)),
            pl.BlockSpec((2, d), lambda i: (0, 0)),
            pl.BlockSpec((2, d), lambda i: (0, 0)),
        ],
        out_specs=pl.BlockSpec((tm, d), lambda i: (i, 0)),
        out_shape=jax.ShapeDtypeStruct((m, d), BF16),
        compiler_params=_cparams("parallel"),
        name="norm_mod",
    )(x, g.reshape(1, d), sh, sc)


def _proj_kernel(a_ref, w_ref, o_ref):
    o_ref[...] = jnp.dot(a_ref[...], w_ref[...].astype(BF16),
                         preferred_element_type=F32).astype(o_ref.dtype)


def _proj(a, w, layer, out_dtype, tm, tn):
    m, k = a.shape
    n = w.shape[2]
    return pl.pallas_call(
        _proj_kernel,
        grid=(m // tm, n // tn),
        in_specs=[
            pl.BlockSpec((tm, k), lambda i, j: (i, 0)),
            pl.BlockSpec((None, k, tn), lambda i, j: (layer, 0, j)),
        ],
        out_specs=pl.BlockSpec((tm, tn), lambda i, j: (i, j)),
        out_shape=jax.ShapeDtypeStruct((m, n), out_dtype),
        compiler_params=_cparams("parallel", "arbitrary"),
        name="proj",
    )(a, w)


def _gate_rows(gate_ref, tm, n_lat):
    rows = pl.program_id(0) * tm + lax.broadcasted_iota(jnp.int32, (tm, 1), 0)
    return jnp.where(rows < n_lat, gate_ref[0:1, :], gate_ref[1:2, :])


def _mix_out_kernel(a0_ref, a1_ref, a2_ref, w_ref, res_ref, gate_ref, o_ref, *, tm, n_lat):
    k0 = a0_ref.shape[1]
    k1 = a1_ref.shape[1]
    k2 = a2_ref.shape[1]
    acc = jnp.dot(a0_ref[...], w_ref[0:k0, :].astype(BF16), preferred_element_type=F32)
    acc += jnp.dot(a1_ref[...], w_ref[k0:k0 + k1, :].astype(BF16), preferred_element_type=F32)
    acc += jnp.dot(a2_ref[...], w_ref[k0 + k1:k0 + k1 + k2, :].astype(BF16),
                   preferred_element_type=F32)
    o_ref[...] = res_ref[...] + _gate_rows(gate_ref, tm, n_lat) * acc


def _mix_out(a0, a1, a2, w, layer, res, gate, n_lat, m, tm, tn):
    _, k, n = w.shape
    return pl.pallas_call(
        functools.partial(_mix_out_kernel, tm=tm, n_lat=n_lat),
        grid=(m // tm, n // tn),
        in_specs=[
            pl.BlockSpec((tm, a0.shape[1]), lambda i, j: (i, 0)),
            pl.BlockSpec((tm, a1.shape[1]), lambda i, j: (i, 0)),
            pl.BlockSpec((tm, a2.shape[1]), lambda i, j: (i, 0)),
            pl.BlockSpec((None, k, tn), lambda i, j: (layer, 0, j)),
            pl.BlockSpec((tm, tn), lambda i, j: (i, j)),
            pl.BlockSpec((2, tn), lambda i, j: (0, j)),
        ],
        out_specs=pl.BlockSpec((tm, tn), lambda i, j: (i, j)),
        out_shape=jax.ShapeDtypeStruct((m, n), F32),
        compiler_params=_cparams("parallel", "arbitrary"),
        name="mix_out",
    )(a0, a1, a2, w, res, gate)


def _ffn_up_kernel(a_ref, w1_ref, w3_ref, o_ref):
    a = a_ref[...]
    t = jnp.dot(a, w1_ref[...].astype(BF16), preferred_element_type=F32)
    u = jnp.dot(a, w3_ref[...].astype(BF16), preferred_element_type=F32)
    o_ref[...] = (_silu(t) * u).astype(o_ref.dtype)


def _ffn_up(a, w1, w3, layer, tm, tn):
    m, k = a.shape
    n = w1.shape[2]
    return pl.pallas_call(
        _ffn_up_kernel,
        grid=(m // tm, n // tn),
        in_specs=[
            pl.BlockSpec((tm, k), lambda i, j: (i, 0)),
            pl.BlockSpec((None, k, tn), lambda i, j: (layer, 0, j)),
            pl.BlockSpec((None, k, tn), lambda i, j: (layer, 0, j)),
        ],
        out_specs=pl.BlockSpec((tm, tn), lambda i, j: (i, j)),
        out_shape=jax.ShapeDtypeStruct((m, n), BF16),
        compiler_params=_cparams("parallel", "arbitrary"),
        name="ffn_up",
    )(a, w1, w3)


def _ffn_down_kernel(a_ref, w_ref, res_ref, gate_ref, o_ref, *, tm, n_lat):
    acc = jnp.dot(a_ref[...], w_ref[...], preferred_element_type=F32)
    o_ref[...] = res_ref[...] + _gate_rows(gate_ref, tm, n_lat) * acc


def _ffn_down(a, w, layer, res, gate, n_lat, tm, tn):
    m, k = a.shape
    n = w.shape[2]
    return pl.pallas_call(
        functools.partial(_ffn_down_kernel, tm=tm, n_lat=n_lat),
        grid=(m // tm, n // tn),
        in_specs=[
            pl.BlockSpec((tm, k), lambda i, j: (i, 0)),
            pl.BlockSpec((None, k, tn), lambda i, j: (layer, 0, j)),
            pl.BlockSpec((tm, tn), lambda i, j: (i, j)),
            pl.BlockSpec((2, tn), lambda i, j: (0, j)),
        ],
        out_specs=pl.BlockSpec((tm, tn), lambda i, j: (i, j)),
        out_shape=jax.ShapeDtypeStruct((m, n), F32),
        compiler_params=_cparams("parallel", "arbitrary"),
        name="ffn_down",
    )(a, w, res, gate)


def _conv_kernel(main_ref, prev_ref, next_ref, cw_ref, cb_ref, lng_ref, lnb_ref, pw_ref, pwb_ref,
                 o_ref, uext_ref, yc_ref, wbf_ref, *, n_lat_tiles, n_tiles):
    i = pl.program_id(0)
    tl = main_ref.shape[0]
    ch = o_ref.shape[1]
    nblk = ch // LANES

    @pl.when(i == 0)
    def _():
        wbf_ref[...] = pw_ref[...].astype(BF16)

    def glu(a):
        return a[:, :ch] * _sigmoid(a[:, ch:])

    is_start = jnp.logical_or(i == 0, i == n_lat_tiles)
    is_end = jnp.logical_or(i == n_lat_tiles - 1, i == n_tiles - 1)
    u = glu(main_ref[...])
    up = jnp.where(is_start, 0.0, glu(prev_ref[...]))
    un = jnp.where(is_end, 0.0, glu(next_ref[...]))
    for c in range(nblk):
        sl = slice(c * LANES, (c + 1) * LANES)
        uext_ref[c, 0:HALO, :] = up[:, sl]
        uext_ref[c, HALO:HALO + tl, :] = u[:, sl]
        uext_ref[c, HALO + tl:2 * HALO + tl, :] = un[:, sl]

    def chan_block(c, carry):
        acc = jnp.zeros((tl, LANES), F32)
        for k in range(CONV_K):
            acc = acc + uext_ref[c, pl.ds(k + HALO - CONV_K // 2, tl), :] * cw_ref[c, k:k + 1, :]
        yc_ref[c] = acc + cb_ref[c]
        return carry

    lax.fori_loop(0, nblk, chan_block, 0)

    y = jnp.concatenate([yc_ref[c] for c in range(nblk)], axis=1)
    mu = jnp.mean(y, axis=-1, keepdims=True)
    yd = y - mu
    var = jnp.mean(yd * yd, axis=-1, keepdims=True)
    yn = yd * lax.rsqrt(var + EPS) * lng_ref[...] + lnb_ref[...]
    s = _silu(yn).astype(BF16)
    o_ref[...] = (jnp.dot(s, wbf_ref[...], preferred_element_type=F32) + pwb_ref[...]).astype(o_ref.dtype)


def _conv_module(z, conv_w, conv_b, ln_g, ln_b, pw_w, layer, pw_b, n_lat):
    m = z.shape[0]
    ch = conv_w.shape[1]
    tl = SEQ_TILE
    nblk = ch // LANES
    n_tiles = m // tl
    hb = tl // HALO
    cw = conv_w.reshape(CONV_K, nblk, LANES).transpose(1, 0, 2)
    cb = conv_b.reshape(nblk, 1, LANES)
    return pl.pallas_call(
        functools.partial(_conv_kernel, n_lat_tiles=n_lat // tl, n_tiles=n_tiles),
        grid=(n_tiles,),
        in_specs=[
            pl.BlockSpec((tl, 2 * ch), lambda i: (i, 0)),
            pl.BlockSpec((HALO, 2 * ch), lambda i: (jnp.maximum(i * hb - 1, 0), 0)),
            pl.BlockSpec((HALO, 2 * ch), lambda i: (jnp.minimum((i + 1) * hb, n_tiles * hb - 1), 0)),
            pl.BlockSpec((nblk, CONV_K, LANES), lambda i: (0, 0, 0)),
            pl.BlockSpec((nblk, 1, LANES), lambda i: (0, 0, 0)),
            pl.BlockSpec((1, ch), lambda i: (0, 0)),
            pl.BlockSpec((1, ch), lambda i: (0, 0)),
            pl.BlockSpec((None, ch, ch), lambda i: (layer, 0, 0)),
            pl.BlockSpec((1, ch), lambda i: (0, 0)),
        ],
        out_specs=pl.BlockSpec((tl, ch), lambda i: (i, 0)),
        out_shape=jax.ShapeDtypeStruct((m, ch), BF16),
        scratch_shapes=[
            pltpu.VMEM((nblk, tl + 2 * HALO, LANES), F32),
            pltpu.VMEM((nblk, tl, LANES), F32),
            pltpu.VMEM((ch, ch), BF16),
        ],
        compiler_params=_cparams("arbitrary"),
        name="conv_module",
    )(z, z, z, cw, cb, ln_g.reshape(1, ch), ln_b.reshape(1, ch), pw_w, pw_b.reshape(1, ch))


Q_ROWS = 4
BAND_ROWS = Q_ROWS + WIN_H
_BLOCK_CASES = (
    (0, lambda a: 0),
    (-WIN_H // 2, lambda a: a),
    (-WIN_H, lambda a: WIN_H // 2),
)


def _bias_table_kernel(rpb_ref, o_ref):
    h = pl.program_id(0)
    qc = lax.broadcasted_iota(jnp.int32, (GRID_W, GRID_W), 0)
    kc = lax.broadcasted_iota(jnp.int32, (GRID_W, GRID_W), 1)
    qstart = jnp.clip(qc - WIN_W // 2, 0, GRID_W - WIN_W)
    valid = jnp.logical_and(kc >= qstart, kc < qstart + WIN_W)
    dc = jnp.where(valid, kc - qc + WIN_W - 1, -1)
    tabs = []
    for dr in range(2 * WIN_H - 1):
        t = jnp.full((GRID_W, GRID_W), NEG_INF, F32)
        for d in range(2 * WIN_W - 1):
            t = jnp.where(dc == d, rpb_ref[h, dr, d] * LOG2E, t)
        tabs.append(t)
    masked = jnp.full((GRID_W, GRID_W), NEG_INF, F32)
    for case, (band0, first_valid) in enumerate(_BLOCK_CASES):
        for a in range(Q_ROWS):
            blocks = []
            for i in range(BAND_ROWS):
                in_window = first_valid(a) <= i < first_valid(a) + WIN_H
                dr = band0 + i - a + WIN_H - 1
                blocks.append(tabs[dr] if in_window else masked)
            o_ref[0, case, a * GRID_W:(a + 1) * GRID_W, :] = jnp.concatenate(blocks, axis=1)


def _bias_table(rpb):
    heads = rpb.shape[0]
    shape = (len(_BLOCK_CASES), Q_ROWS * GRID_W, BAND_ROWS * GRID_W)
    return pl.pallas_call(
        _bias_table_kernel,
        grid=(heads,),
        in_specs=[pl.BlockSpec(memory_space=pltpu.SMEM)],
        out_specs=pl.BlockSpec((1,) + shape, lambda h: (h, 0, 0, 0)),
        out_shape=jax.ShapeDtypeStruct((heads,) + shape, F32),
        compiler_params=_cparams("parallel"),
        name="bias_table",
    )(rpb)


def _attn_kernel(q_ref, k_ref, v_ref, qg_ref, kg_ref, bias_ref, o_ref, kn_ref, vb_ref, *, rows):
    m = q_ref.shape[0]
    scale = HEAD_DIM ** -0.5
    band = BAND_ROWS * GRID_W
    nt = (((1,), (1,)), ((), ()))

    def rms(x, g):
        return x * lax.rsqrt(jnp.mean(x * x, axis=-1, keepdims=True) + EPS) * g

    def prep(c, carry):
        r = pl.multiple_of(c * SEQ_TILE, SEQ_TILE)
        kn_ref[pl.ds(r, SEQ_TILE), :] = rms(k_ref[pl.ds(r, SEQ_TILE), :], kg_ref[...]).astype(BF16)
        vb_ref[pl.ds(r, SEQ_TILE), :] = v_ref[pl.ds(r, SEQ_TILE), :].astype(BF16)
        return carry

    lax.fori_loop(0, m // SEQ_TILE, prep, 0)

    n_lat = rows * GRID_W
    kc = kn_ref[n_lat:m, :]
    vc = vb_ref[n_lat:m, :]

    qc = rms(q_ref[n_lat:m, :], qg_ref[...]).astype(BF16)
    s = lax.dot_general(qc, kc, nt, preferred_element_type=F32) * scale
    p = jnp.exp(s - jnp.max(s, axis=-1, keepdims=True))
    l = jnp.sum(p, axis=-1, keepdims=True)
    o_ref[n_lat:m, :] = (jnp.dot(p.astype(BF16), vc, preferred_element_type=F32) / l).astype(o_ref.dtype)

    n_blocks = rows // Q_ROWS
    nq = Q_ROWS * GRID_W

    def block(bq, carry):
        r = bq * Q_ROWS
        band0 = jnp.clip(r - WIN_H // 2, 0, rows - BAND_ROWS)
        case = jnp.where(bq == 0, 0, jnp.where(bq == n_blocks - 1, 2, 1))
        qs = pl.multiple_of(r * GRID_W, nq)
        ks = pl.multiple_of(band0 * GRID_W, nq)
        q = rms(q_ref[pl.ds(qs, nq), :], qg_ref[...]).astype(BF16)
        kb = kn_ref[pl.ds(ks, band), :]
        s_loc = lax.dot_general(q, kb, nt, preferred_element_type=F32) * (scale * LOG2E) + bias_ref[0, case]
        s_ctx = lax.dot_general(q, kc, nt, preferred_element_type=F32) * (scale * LOG2E)
        mx = jnp.maximum(jnp.max(s_loc, axis=-1, keepdims=True), jnp.max(s_ctx, axis=-1, keepdims=True))
        p_loc = jnp.exp2(s_loc - mx)
        p_ctx = jnp.exp2(s_ctx - mx)
        den = jnp.sum(p_loc, axis=-1, keepdims=True) + jnp.sum(p_ctx, axis=-1, keepdims=True)
        o = jnp.dot(p_loc.astype(BF16), vb_ref[pl.ds(ks, band), :], preferred_element_type=F32)
        o = o + jnp.dot(p_ctx.astype(BF16), vc, preferred_element_type=F32)
        o_ref[pl.ds(qs, nq), :] = (o / den).astype(o_ref.dtype)
        return carry

    lax.fori_loop(0, n_blocks, block, 0, unroll=2)


def _attention(z, q_g, k_g, bias_tab, n_lat, col0, width):
    m = z.shape[0]
    heads = width // HEAD_DIM
    rows = n_lat // GRID_W
    assert rows % Q_ROWS == 0 and rows >= 2 * BAND_ROWS and BAND_ROWS % Q_ROWS == 0
    cb = col0 // HEAD_DIM
    return pl.pallas_call(
        functools.partial(_attn_kernel, rows=rows),
        grid=(heads,),
        in_specs=[
            pl.BlockSpec((m, HEAD_DIM), lambda h: (0, cb + h)),
            pl.BlockSpec((m, HEAD_DIM), lambda h: (0, cb + heads + h)),
            pl.BlockSpec((m, HEAD_DIM), lambda h: (0, cb + 2 * heads + h)),
            pl.BlockSpec((1, HEAD_DIM), lambda h: (0, 0)),
            pl.BlockSpec((1, HEAD_DIM), lambda h: (0, 0)),
            pl.BlockSpec((1,) + bias_tab.shape[1:], lambda h: (h, 0, 0, 0)),
        ],
        out_specs=pl.BlockSpec((m, HEAD_DIM), lambda h: (0, h)),
        out_shape=jax.ShapeDtypeStruct((m, width), BF16),
        scratch_shapes=[pltpu.VMEM((m, HEAD_DIM), BF16), pltpu.VMEM((m, HEAD_DIM), BF16)],
        compiler_params=_cparams("parallel"),
        name="nbr_attention",
    )(z, z, z, q_g.reshape(1, HEAD_DIM), k_g.reshape(1, HEAD_DIM), bias_tab)


def _discretize_kernel(lr_ref, li_ref, ls_ref, br_ref, bi_ref, ar_ref, ai_ref, bbr_ref, bbi_ref):
    lr = lr_ref[0]
    li = li_ref[0]
    step = jnp.exp(ls_ref[0])
    mag = jnp.exp(lr * step)
    ang = li * step
    a_re = mag * jnp.cos(ang)
    a_im = mag * jnp.sin(ang)
    inv = 1.0 / (lr * lr + li * li)
    f_re = ((a_re - 1.0) * lr + a_im * li) * inv
    f_im = (a_im * lr - (a_re - 1.0) * li) * inv
    br = br_ref[0]
    bi = bi_ref[0]
    ar_ref[0] = a_re
    ai_ref[0] = a_im
    bbr_ref[0] = f_re * br - f_im * bi
    bbi_ref[0] = f_re * bi + f_im * br


def _discretize(lam_re, lam_im, log_step, b_re, b_im):
    n_dir, g, n = lam_re.shape
    p = b_re.shape[-1]
    rep = lambda a: jnp.repeat(a, p, axis=1)
    lr = rep(lam_re)
    li = rep(lam_im)
    ls = rep(jnp.broadcast_to(log_step[:, :, None], (n_dir, g, n)))
    br = b_re.transpose(0, 1, 3, 2).reshape(n_dir, g * p, n)
    bi = b_im.transpose(0, 1, 3, 2).reshape(n_dir, g * p, n)
    spec = pl.BlockSpec((1, g * p, n), lambda d: (d, 0, 0))
    shp = jax.ShapeDtypeStruct((n_dir, g * p, n), F32)
    return pl.pallas_call(
        _discretize_kernel,
        grid=(n_dir,),
        in_specs=[spec] * 5,
        out_specs=[spec] * 4,
        out_shape=[shp] * 4,
        compiler_params=_cparams("parallel"),
        name="s5_discretize",
    )(lr, li, ls, br, bi)


def _scan_weights(a_re, a_im, bbt_re, bbt_im, c_re, c_im):
    g, n = a_re.shape[0] // SSM_GROUP, a_re.shape[1]
    gb = LANES // SSM_GROUP
    nb = g // gb
    eye = jnp.eye(gb, dtype=F32)
    ar = a_re[::SSM_GROUP].reshape(g * n // LANES, LANES)
    ai = a_im[::SSM_GROUP].reshape(g * n // LANES, LANES)

    def in_mat(bbt):
        blk = bbt.reshape(nb, gb, SSM_GROUP, n)
        return jnp.einsum("bgpn,gh->bgphn", blk, eye).reshape(nb, LANES, gb * n)

    def out_mat(c):
        blk = c.reshape(nb, gb, SSM_GROUP, n)
        return jnp.einsum("bgpn,gh->bgnhp", blk, eye).reshape(nb, gb * n, LANES)

    wb = jnp.concatenate([in_mat(bbt_re), in_mat(bbt_im)], axis=2).astype(BF16)
    wc = jnp.concatenate([out_mat(c_re), -out_mat(c_im)], axis=1).astype(BF16)
    return ar, ai, wb, wc


def _scan_chunk(u_ref, wb_ref, ar_ref, ai_ref, wc_ref, xs_ref, hs_ref, reverse):
    t_len = u_ref.shape[0]
    nb = wb_ref.shape[0]
    half = wb_ref.shape[2] // 2
    spb = half // LANES
    n_re = nb * spb
    u = u_ref[...].astype(BF16)
    for b in range(nb):
        xb = jnp.dot(u[:, b * LANES:(b + 1) * LANES], wb_ref[b], preferred_element_type=F32)
        for k in range(spb):
            xs_ref[pl.ds((b * spb + k) * SCAN_PITCH, t_len), :] = xb[:, k * LANES:(k + 1) * LANES]
            xs_ref[pl.ds((n_re + b * spb + k) * SCAN_PITCH, t_len), :] = (
                xb[:, half + k * LANES:half + (k + 1) * LANES])

    ar = ar_ref[...]
    ai = ai_ref[...]
    im0 = n_re * SCAN_PITCH

    def step(j, carry):
        h_re, h_im = carry
        t = (t_len - 1 - j) if reverse else j
        x_re = xs_ref[pl.ds(t, n_re, stride=SCAN_PITCH), :]
        x_im = xs_ref[pl.ds(im0 + t, n_re, stride=SCAN_PITCH), :]
        n_h_re = ar * h_re - ai * h_im + x_re
        n_h_im = ar * h_im + ai * h_re + x_im
        xs_ref[pl.ds(t, n_re, stride=SCAN_PITCH), :] = n_h_re
        xs_ref[pl.ds(im0 + t, n_re, stride=SCAN_PITCH), :] = n_h_im
        return n_h_re, n_h_im

    h_re, h_im = lax.fori_loop(0, t_len, step, (hs_ref[0], hs_ref[1]), unroll=8)
    hs_ref[0] = h_re
    hs_ref[1] = h_im

    ys = []
    for b in range(nb):
        slabs = [xs_ref[pl.ds((b * spb + k) * SCAN_PITCH, t_len), :] for k in range(spb)]
        slabs += [xs_ref[pl.ds((n_re + b * spb + k) * SCAN_PITCH, t_len), :] for k in range(spb)]
        hb = jnp.concatenate(slabs, axis=1).astype(BF16)
        ys.append(jnp.dot(hb, wc_ref[b], preferred_element_type=F32))
    return jnp.concatenate(ys, axis=1)


def _scan_fwd_kernel(u_ref, wb_ref, ar_ref, ai_ref, wc_ref, o_ref, xs_ref, hs_ref):
    @pl.when(pl.program_id(0) == 0)
    def _():
        hs_ref[...] = jnp.zeros_like(hs_ref)

    o_ref[...] = _scan_chunk(u_ref, wb_ref, ar_ref, ai_ref, wc_ref, xs_ref, hs_ref, False)


def _scan_bwd_kernel(u_ref, wb_ref, ar_ref, ai_ref, wc_ref, yf_ref, d_ref, gw_ref, gb_ref, o_ref,
                     xs_ref, hs_ref, gwbf_ref):
    @pl.when(pl.program_id(0) == 0)
    def _():
        hs_ref[...] = jnp.zeros_like(hs_ref)
        gwbf_ref[...] = gw_ref[...].astype(BF16)

    y = _scan_chunk(u_ref, wb_ref, ar_ref, ai_ref, wc_ref, xs_ref, hs_ref, True)
    y = y + yf_ref[...] + u_ref[...] * d_ref[...]
    g = jax.nn.gelu(y)
    gate = jnp.dot(g.astype(BF16), gwbf_ref[...], preferred_element_type=F32) + gb_ref[...]
    o_ref[...] = (g * _sigmoid(gate)).astype(o_ref.dtype)


def _s5_mixer(z, col0, ch, disc, c_re, c_im, d_skip, glu_w, layer, glu_b, n_lat):
    m = z.shape[0]
    t_len = SEQ_TILE
    assert m - n_lat == t_len
    n_chunks = m // t_len
    ctx_chunk = n_chunks - 1
    cb = col0 // ch
    a_re, a_im, bbt_re, bbt_im = disc
    w_f = _scan_weights(a_re[0], a_im[0], bbt_re[0], bbt_im[0], c_re[0], c_im[0])
    w_b = _scan_weights(a_re[1], a_im[1], bbt_re[1], bbt_im[1], c_re[1], c_im[1])
    ar, ai, wb, wc = w_f
    nb = wb.shape[0]
    n_slab = 2 * ar.shape[0]

    def const_spec(a):
        nd = a.ndim
        return pl.BlockSpec(a.shape, lambda i: (0,) * nd)

    scratch = [pltpu.VMEM((n_slab * SCAN_PITCH, LANES), F32), pltpu.VMEM((2,) + ar.shape, F32)]

    chunk = lambda i: jnp.where(i == 0, ctx_chunk, i - 1)
    y_f = pl.pallas_call(
        _scan_fwd_kernel,
        grid=(n_chunks,),
        in_specs=[pl.BlockSpec((t_len, ch), lambda i: (chunk(i), cb))]
        + [const_spec(a) for a in (wb, ar, ai, wc)],
        out_specs=pl.BlockSpec((t_len, ch), lambda i: (chunk(i), 0)),
        out_shape=jax.ShapeDtypeStruct((m, ch), F32),
        scratch_shapes=scratch,
        compiler_params=_cparams("arbitrary"),
        name="s5_scan_fwd",
    )(z, wb, ar, ai, wc)

    ar, ai, wb, wc = w_b
    chunk = lambda i: jnp.where(i == 0, ctx_chunk, ctx_chunk - i)
    return pl.pallas_call(
        _scan_bwd_kernel,
        grid=(n_chunks,),
        in_specs=[pl.BlockSpec((t_len, ch), lambda i: (chunk(i), cb))]
        + [const_spec(a) for a in (wb, ar, ai, wc)]
        + [
            pl.BlockSpec((t_len, ch), lambda i: (chunk(i), 0)),
            pl.BlockSpec((1, ch), lambda i: (0, 0)),
            pl.BlockSpec((None, ch, ch), lambda i: (layer, 0, 0)),
            pl.BlockSpec((1, ch), lambda i: (0, 0)),
        ],
        out_specs=pl.BlockSpec((t_len, ch), lambda i: (chunk(i), 0)),
        out_shape=jax.ShapeDtypeStruct((m, ch), BF16),
        scratch_shapes=scratch + [pltpu.VMEM((ch, ch), BF16)],
        compiler_params=_cparams("arbitrary"),
        name="s5_scan_bwd",
    )(z, wb, ar, ai, wc, y_f, d_skip.reshape(1, ch), glu_w, glu_b.reshape(1, ch))


def _layer(xa, cc, n_lat, layer, last, w_mod, b_mod, g_norm1, g_norm2, w_in, conv_w, conv_b, conv_ln_g,
           conv_ln_b, conv_pw_w, conv_pw_b, q_norm_g, k_norm_g, rpb, lam_re, lam_im, log_step, b_re, b_im,
           c_re, c_im, ssm_d, glu_w, glu_b, w_out, w_ff1, w_ff3, w_ff2):
    m_all, d = xa.shape
    conv_ch = conv_w.shape[1]
    ssm_ch = ssm_d.shape[0]
    na_width = (w_in.shape[2] - 2 * conv_ch - ssm_ch) // 3
    o1 = 2 * conv_ch
    o4 = o1 + 3 * na_width
    m_out = n_lat if last else m_all
    tm_all = m_all // 8
    tm_out = m_out // 8

    mod = _modulation(cc, w_mod, layer, b_mod)[:2]
    sh1, sc1, g1, sh2, sc2, g2 = [mod[:, i * d:(i + 1) * d] for i in range(6)]

    z = _proj(_norm_mod(xa, g_norm1, sh1, sc1, n_lat, m_all), w_in, layer, F32, tm_all, 512)

    y_conv = _conv_module(z, conv_w, conv_b, conv_ln_g, conv_ln_b, conv_pw_w, layer, conv_pw_b, n_lat)
    y_na = _attention(z, q_norm_g, k_norm_g, _bias_table(rpb), n_lat, o1, na_width)
    disc = _discretize(lam_re, lam_im, log_step, b_re, b_im)
    y_s5 = _s5_mixer(z, o4, ssm_ch, disc, c_re, c_im, ssm_d, glu_w, layer, glu_b, n_lat)

    xa = _mix_out(y_conv, y_na, y_s5, w_out, layer, xa, g1, n_lat, m_out, tm_out, 512)
    hidden = _ffn_up(_norm_mod(xa, g_norm2, sh2, sc2, n_lat, m_out), w_ff1, w_ff3, layer, tm_out, 256)
    return _ffn_down(hidden, w_ff2, layer, xa, g2, n_lat, tm_out // 2, 512)


def kernel(x, c, ctx, c_ctx, w_mod, b_mod, g_norm1, g_norm2, w_in, conv_w, conv_b, conv_ln_g, conv_ln_b,
           conv_pw_w, conv_pw_b, q_norm_g, k_norm_g, rpb, ssm_lambda_re, ssm_lambda_im, ssm_log_step,
           ssm_b_re, ssm_b_im, ssm_c_re, ssm_c_im, ssm_d, ssm_glu_w, ssm_glu_b, w_out, w_ff1, w_ff3, w_ff2):
    batch, seq, d = x.shape
    n_ctx = ctx.shape[1]
    assert batch == 1 and n_ctx == SEQ_TILE and seq % GRID_W == 0
    depth = w_mod.shape[0]
    xa = jnp.concatenate([x[0], ctx[0]], axis=0)
    cc = jnp.zeros((SUBLANES, d), F32).at[0].set(c[0]).at[1].set(c_ctx)
    w_ff2 = w_ff2.astype(BF16)
    for l in range(depth):
        xa = _layer(xa, cc, seq, l, l == depth - 1, w_mod, b_mod[l], g_norm1[l], g_norm2[l], w_in, conv_w[l],
                    conv_b[l], conv_ln_g[l], conv_ln_b[l], conv_pw_w, conv_pw_b[l], q_norm_g[l], k_norm_g[l],
                    rpb[l], ssm_lambda_re[l], ssm_lambda_im[l], ssm_log_step[l], ssm_b_re[l], ssm_b_im[l],
                    ssm_c_re[l], ssm_c_im[l], ssm_d[l], ssm_glu_w, ssm_glu_b[l], w_out, w_ff1,
                    w_ff3, w_ff2)
    return xa[None]
```

```python
import functools

import jax
import jax.numpy as jnp
from jax import lax
from jax.experimental import pallas as pl
from jax.experimental.pallas import tpu as pltpu

F32 = jnp.float32
BF16 = jnp.bfloat16

GRID_W = 64
WIN_H = 8
WIN_W = 16
HEAD_DIM = 128
CONV_K = 31
SSM_GROUP = 16
SSM_STATE = 64
EPS = 1e-6
NEG_INF = -1e30
LOG2E = 1.4426950408889634

LANES = 128
SUBLANES = 8
VMEM_LIMIT = 56 * 1024 * 1024

SEQ_TILE = 256
SCAN_PITCH = SEQ_TILE + SUBLANES
PIECE = 2 * LANES
HALO = 16


def _cparams(*sem):
    return pltpu.CompilerParams(dimension_semantics=sem, vmem_limit_bytes=VMEM_LIMIT)


def _sigmoid(x):
    return 1.0 / (1.0 + jnp.exp(-x))


def _silu(x):
    return x * _sigmoid(x)


def _mod_kernel(c_ref, w_ref, b_ref, o_ref):
    s = _silu(c_ref[...]).astype(BF16)
    o_ref[...] = jnp.dot(s, w_ref[...].astype(BF16), preferred_element_type=F32) + b_ref[...]


def _modulation(cc, w_mod, layer, b_mod):
    _, d, n = w_mod.shape
    tn = 512
    return pl.pallas_call(
        _mod_kernel,
        grid=(n // tn,),
        in_specs=[
            pl.BlockSpec((SUBLANES, d), lambda j: (0, 0)),
            pl.BlockSpec((None, d, tn), lambda j: (layer, 0, j)),
            pl.BlockSpec((1, tn), lambda j: (0, j)),
        ],
        out_specs=pl.BlockSpec((SUBLANES, tn), lambda j: (0, j)),
        out_shape=jax.ShapeDtypeStruct((SUBLANES, n), F32),
        compiler_params=_cparams("parallel"),
        name="adaln_mod",
    )(cc, w_mod, b_mod.reshape(1, n))


def _norm_mod_kernel(x_ref, g_ref, sh_ref, sc_ref, o_ref, *, n_lat_tiles):
    i = pl.program_id(0)
    x = x_ref[...]
    y = x * lax.rsqrt(jnp.mean(x * x, axis=-1, keepdims=True) + EPS) * g_ref[...]
    row = jnp.where(i >= n_lat_tiles, 1, 0)
    sh = sh_ref[pl.ds(row, 1), :]
    sc = sc_ref[pl.ds(row, 1), :]
    o_ref[...] = (y * (1.0 + sc) + sh).astype(BF16)


def _stack_norm_mod_kernel(lat_ref, ctx_ref, g_ref, sh_ref, sc_ref, o_ref, xa_ref, *, n_lat_tiles):
    i = pl.program_id(0)

    def emit(x, row):
        xa_ref[...] = x
        y = x * lax.rsqrt(jnp.mean(x * x, axis=-1, keepdims=True) + EPS) * g_ref[...]
        o_ref[...] = (y * (1.0 + sc_ref[row:row + 1, :]) + sh_ref[row:row + 1, :]).astype(BF16)

    @pl.when(i < n_lat_tiles)
    def _():
        emit(lat_ref[...], 0)

    @pl.when(i >= n_lat_tiles)
    def _():
        emit(ctx_ref[...], 1)


def _stack_norm_mod(lat, ctx, g, sh, sc):
    n_lat, d = lat.shape
    tm = SEQ_TILE
    n_lat_tiles = n_lat // tm
    m = n_lat + ctx.shape[0]
    return pl.pallas_call(
        functools.partial(_stack_norm_mod_kernel, n_lat_tiles=n_lat_tiles),
        grid=(m // tm,),
        in_specs=[
            pl.BlockSpec((tm, d), lambda i: (jnp.minimum(i, n_lat_tiles - 1), 0)),
            pl.BlockSpec((tm, d), lambda i: (jnp.maximum(i - n_lat_tiles, 0), 0)),
            pl.BlockSpec((1, d), lambda i: (0, 0)),
            pl.BlockSpec((2, d), lambda i: (0, 0)),
            pl.BlockSpec((2, d), lambda i: (0, 0)),
        ],
        out_specs=[pl.BlockSpec((tm, d), lambda i: (i, 0)), pl.BlockSpec((tm, d), lambda i: (i, 0))],
        out_shape=[jax.ShapeDtypeStruct((m, d), BF16), jax.ShapeDtypeStruct((m, d), F32)],
        compiler_params=_cparams("parallel"),
        name="stack_norm_mod",
    )(lat, ctx, g.reshape(1, d), sh, sc)


def _norm_mod(x, g, sh, sc, n_lat, m):
    d = x.shape[1]
    tm = SEQ_TILE
    return pl.pallas_call(
        functools.partial(_norm_mod_kernel, n_lat_tiles=n_lat // tm),
        grid=(m // tm,),
        in_specs=[
            pl.BlockSpec((tm, d), lambda i: (i, 0)),
            pl.BlockSpec((1, d), lambda i: (0, 0)),
            pl.BlockSpec((2, d), lambda i: (0, 0)),
            pl.BlockSpec((2, d), lambda i: (0, 0)),
        ],
        out_specs=pl.BlockSpec((tm, d), lambda i: (i, 0)),
        out_shape=jax.ShapeDtypeStruct((m, d), BF16),
        compiler_params=_cparams("parallel"),
        name="norm_mod",
    )(x, g.reshape(1, d), sh, sc)


def _proj_kernel(a_ref, w_ref, o_ref):
    o_ref[...] = jnp.dot(a_ref[...], w_ref[...].astype(BF16),
                         preferred_element_type=F32).astype(o_ref.dtype)


def _proj(a, w, layer, out_dtype, tm, tn):
    m, k = a.shape
    n = w.shape[2]
    return pl.pallas_call(
        _proj_kernel,
        grid=(m // tm, n // tn),
        in_specs=[
            pl.BlockSpec((tm, k), lambda i, j: (i, 0)),
            pl.BlockSpec((None, k, tn), lambda i, j: (layer, 0, j)),
        ],
        out_specs=pl.BlockSpec((tm, tn), lambda i, j: (i, j)),
        out_shape=jax.ShapeDtypeStruct((m, n), out_dtype),
        compiler_params=_cparams("parallel", "arbitrary"),
        name="proj",
    )(a, w)


def _gate_rows(gate_ref, tm, n_lat):
    rows = pl.program_id(0) * tm + lax.broadcasted_iota(jnp.int32, (tm, 1), 0)
    return jnp.where(rows < n_lat, gate_ref[0:1, :], gate_ref[1:2, :])


def _mix_out_kernel(a0_ref, a1_ref, a2_ref, w_ref, res_ref, gate_ref, o_ref, *, tm, n_lat):
    k0 = a0_ref.shape[1]
    k1 = a1_ref.shape[1]
    k2 = a2_ref.shape[1]
    acc = jnp.dot(a0_ref[...], w_ref[0:k0, :].astype(BF16), preferred_element_type=F32)
    acc += jnp.dot(a1_ref[...], w_ref[k0:k0 + k1, :].astype(BF16), preferred_element_type=F32)
    acc += jnp.dot(a2_ref[...], w_ref[k0 + k1:k0 + k1 + k2, :].astype(BF16),
                   preferred_element_type=F32)
    o_ref[...] = res_ref[...] + _gate_rows(gate_ref, tm, n_lat) * acc


def _mix_out(a0, a1, a2, w, layer, res, gate, n_lat, m, tm, tn):
    _, k, n = w.shape
    return pl.pallas_call(
        functools.partial(_mix_out_kernel, tm=tm, n_lat=n_lat),
        grid=(m // tm, n // tn),
        in_specs=[
            pl.BlockSpec((tm, a0.shape[1]), lambda i, j: (i, 0)),
            pl.BlockSpec((tm, a1.shape[1]), lambda i, j: (i, 0)),
            pl.BlockSpec((tm, a2.shape[1]), lambda i, j: (i, 0)),
            pl.BlockSpec((None, k, tn), lambda i, j: (layer, 0, j)),
            pl.BlockSpec((tm, tn), lambda i, j: (i, j)),
            pl.BlockSpec((2, tn), lambda i, j: (0, j)),
        ],
        out_specs=pl.BlockSpec((tm, tn), lambda i, j: (i, j)),
        out_shape=jax.ShapeDtypeStruct((m, n), F32),
        compiler_params=_cparams("parallel", "arbitrary"),
        name="mix_out",
    )(a0, a1, a2, w, res, gate)


def _ffn_up_kernel(a_ref, w1_ref, w3_ref, o_ref):
    a = a_ref[...]
    t = jnp.dot(a, w1_ref[...].astype(BF16), preferred_element_type=F32)
    u = jnp.dot(a, w3_ref[...].astype(BF16), preferred_element_type=F32)
    o_ref[...] = (_silu(t) * u).astype(o_ref.dtype)


def _ffn_up(a, w1, w3, layer, tm, tn):
    m, k = a.shape
    n = w1.shape[2]
    return pl.pallas_call(
        _ffn_up_kernel,
        grid=(m // tm, n // tn),
        in_specs=[
            pl.BlockSpec((tm, k), lambda i, j: (i, 0)),
            pl.BlockSpec((None, k, tn), lambda i, j: (layer, 0, j)),
            pl.BlockSpec((None, k, tn), lambda i, j: (layer, 0, j)),
        ],
        out_specs=pl.BlockSpec((tm, tn), lambda i, j: (i, j)),
        out_shape=jax.ShapeDtypeStruct((m, n), BF16),
        compiler_params=_cparams("parallel", "arbitrary"),
        name="ffn_up",
    )(a, w1, w3)


def _ffn_down_kernel(a_ref, w_ref, res_ref, gate_ref, o_ref, *, tm, n_lat):
    acc = jnp.dot(a_ref[...], w_ref[...], preferred_element_type=F32)
    o_ref[...] = res_ref[...] + _gate_rows(gate_ref, tm, n_lat) * acc


def _ffn_down(a, w, layer, res, gate, n_lat, tm, tn):
    m, k = a.shape
    n = w.shape[2]
    return pl.pallas_call(
        functools.partial(_ffn_down_kernel, tm=tm, n_lat=n_lat),
        grid=(m // tm, n // tn),
        in_specs=[
            pl.BlockSpec((tm, k), lambda i, j: (i, 0)),
            pl.BlockSpec((None, k, tn), lambda i, j: (layer, 0, j)),
            pl.BlockSpec((tm, tn), lambda i, j: (i, j)),
            pl.BlockSpec((2, tn), lambda i, j: (0, j)),
        ],
        out_specs=pl.BlockSpec((tm, tn), lambda i, j: (i, j)),
        out_shape=jax.ShapeDtypeStruct((m, n), F32),
        compiler_params=_cparams("parallel", "arbitrary"),
        name="ffn_down",
    )(a, w, res, gate)


def _conv_kernel(main_ref, prev_ref, next_ref, cw_ref, cb_ref, lng_ref, lnb_ref, pw_ref, pwb_ref,
                 o_ref, uext_ref, yc_ref, wbf_ref, *, n_lat_tiles, n_tiles):
    i = pl.program_id(0)
    tl = main_ref.shape[0]
    ch = o_ref.shape[1]
    nblk = ch // LANES

    @pl.when(i == 0)
    def _():
        wbf_ref[...] = pw_ref[...].astype(BF16)

    def glu(a):
        return a[:, :ch] * _sigmoid(a[:, ch:])

    is_start = jnp.logical_or(i == 0, i == n_lat_tiles)
    is_end = jnp.logical_or(i == n_lat_tiles - 1, i == n_tiles - 1)
    u = glu(main_ref[...])
    up = jnp.where(is_start, 0.0, glu(prev_ref[...]))
    un = jnp.where(is_end, 0.0, glu(next_ref[...]))
    for c in range(nblk):
        sl = slice(c * LANES, (c + 1) * LANES)
        uext_ref[c, 0:HALO, :] = up[:, sl]
        uext_ref[c, HALO:HALO + tl, :] = u[:, sl]
        uext_ref[c, HALO + tl:2 * HALO + tl, :] = un[:, sl]

    def chan_block(c, carry):
        acc = jnp.zeros((tl, LANES), F32)
        for k in range(CONV_K):
            acc = acc + uext_ref[c, pl.ds(k + HALO - CONV_K // 2, tl), :] * cw_ref[c, k:k + 1, :]
        yc_ref[c] = acc + cb_ref[c]
        return carry

    lax.fori_loop(0, nblk, chan_block, 0)

    y = jnp.concatenate([yc_ref[c] for c in range(nblk)], axis=1)
    mu = jnp.mean(y, axis=-1, keepdims=True)
    yd = y - mu
    var = jnp.mean(yd * yd, axis=-1, keepdims=True)
    yn = yd * lax.rsqrt(var + EPS) * lng_ref[...] + lnb_ref[...]
    s = _silu(yn).astype(BF16)
    o_ref[...] = (jnp.dot(s, wbf_ref[...], preferred_element_type=F32) + pwb_ref[...]).astype(o_ref.dtype)


def _conv_module(z, conv_w, conv_b, ln_g, ln_b, pw_w, layer, pw_b, n_lat):
    m = z.shape[0]
    ch = conv_w.shape[1]
    tl = SEQ_TILE
    nblk = ch // LANES
    n_tiles = m // tl
    hb = tl // HALO
    cw = conv_w.reshape(CONV_K, nblk, LANES).transpose(1, 0, 2)
    cb = conv_b.reshape(nblk, 1, LANES)
    return pl.pallas_call(
        functools.partial(_conv_kernel, n_lat_tiles=n_lat // tl, n_tiles=n_tiles),
        grid=(n_tiles,),
        in_specs=[
            pl.BlockSpec((tl, 2 * ch), lambda i: (i, 0)),
            pl.BlockSpec((HALO, 2 * ch), lambda i: (jnp.maximum(i * hb - 1, 0), 0)),
            pl.BlockSpec((HALO, 2 * ch), lambda i: (jnp.minimum((i + 1) * hb, n_tiles * hb - 1), 0)),
            pl.BlockSpec((nblk, CONV_K, LANES), lambda i: (0, 0, 0)),
            pl.BlockSpec((nblk, 1, LANES), lambda i: (0, 0, 0)),
            pl.BlockSpec((1, ch), lambda i: (0, 0)),
            pl.BlockSpec((1, ch), lambda i: (0, 0)),
            pl.BlockSpec((None, ch, ch), lambda i: (layer, 0, 0)),
            pl.BlockSpec((1, ch), lambda i: (0, 0)),
        ],
        out_specs=pl.BlockSpec((tl, ch), lambda i: (i, 0)),
        out_shape=jax.ShapeDtypeStruct((m, ch), BF16),
        scratch_shapes=[
            pltpu.VMEM((nblk, tl + 2 * HALO, LANES), F32),
            pltpu.VMEM((nblk, tl, LANES), F32),
            pltpu.VMEM((ch, ch), BF16),
        ],
        compiler_params=_cparams("arbitrary"),
        name="conv_module",
    )(z, z, z, cw, cb, ln_g.reshape(1, ch), ln_b.reshape(1, ch), pw_w, pw_b.reshape(1, ch))


Q_ROWS = 4
BAND_ROWS = Q_ROWS + WIN_H
_BLOCK_CASES = (
    (0, lambda a: max(a - WIN_H // 2, 0)),
    (-WIN_H // 2, lambda a: a),
    (-WIN_H, lambda a: min(a + WIN_H // 2, Q_ROWS)),
)


def _bias_table_kernel(rpb_ref, o_ref):
    h = pl.program_id(0)
    qc = lax.broadcasted_iota(jnp.int32, (GRID_W, GRID_W), 0)
    kc = lax.broadcasted_iota(jnp.int32, (GRID_W, GRID_W), 1)
    qstart = jnp.clip(qc - WIN_W // 2, 0, GRID_W - WIN_W)
    valid = jnp.logical_and(kc >= qstart, kc < qstart + WIN_W)
    dc = jnp.where(valid, kc - qc + WIN_W - 1, -1)
    tabs = []
    for dr in range(2 * WIN_H - 1):
        t = jnp.full((GRID_W, GRID_W), NEG_INF, F32)
        for d in range(2 * WIN_W - 1):
            t = jnp.where(dc == d, rpb_ref[h, dr, d] * LOG2E, t)
        tabs.append(t)
    masked = jnp.full((GRID_W, GRID_W), NEG_INF, F32)
    for case, (band0, first_valid) in enumerate(_BLOCK_CASES):
        for a in range(Q_ROWS):
            blocks = []
            for i in range(BAND_ROWS):
                in_window = first_valid(a) <= i < first_valid(a) + WIN_H
                dr = band0 + i - a + WIN_H - 1
                blocks.append(tabs[dr] if in_window else masked)
            o_ref[0, case, a * GRID_W:(a + 1) * GRID_W, :] = jnp.concatenate(blocks, axis=1)


def _bias_table(rpb):
    heads = rpb.shape[0]
    shape = (len(_BLOCK_CASES), Q_ROWS * GRID_W, BAND_ROWS * GRID_W)
    return pl.pallas_call(
        _bias_table_kernel,
        grid=(heads,),
        in_specs=[pl.BlockSpec(memory_space=pltpu.SMEM)],
        out_specs=pl.BlockSpec((1,) + shape, lambda h: (h, 0, 0, 0)),
        out_shape=jax.ShapeDtypeStruct((heads,) + shape, F32),
        compiler_params=_cparams("parallel"),
        name="bias_table",
    )(rpb)


def _attn_kernel(q_ref, k_ref, v_ref, qg_ref, kg_ref, bias_ref, o_ref, kn_ref, vb_ref, *, rows):
    m = q_ref.shape[0]
    scale = HEAD_DIM ** -0.5
    band = BAND_ROWS * GRID_W
    nt = (((1,), (1,)), ((), ()))

    def rms(x, g):
        return x * lax.rsqrt(jnp.mean(x * x, axis=-1, keepdims=True) + EPS) * g

    def prep(c, carry):
        r = pl.multiple_of(c * SEQ_TILE, SEQ_TILE)
        kn_ref[pl.ds(r, SEQ_TILE), :] = rms(k_ref[pl.ds(r, SEQ_TILE), :], kg_ref[...]).astype(BF16)
        vb_ref[pl.ds(r, SEQ_TILE), :] = v_ref[pl.ds(r, SEQ_TILE), :].astype(BF16)
        return carry

    lax.fori_loop(0, m // SEQ_TILE, prep, 0)

    n_lat = rows * GRID_W
    kc = kn_ref[n_lat:m, :]
    vc = vb_ref[n_lat:m, :]

    qc = rms(q_ref[n_lat:m, :], qg_ref[...]).astype(BF16)
    s = lax.dot_general(qc, kc, nt, preferred_element_type=F32) * scale
    p = jnp.exp(s - jnp.max(s, axis=-1, keepdims=True))
    l = jnp.sum(p, axis=-1, keepdims=True)
    o_ref[n_lat:m, :] = (jnp.dot(p.astype(BF16), vc, preferred_element_type=F32) / l).astype(o_ref.dtype)

    n_blocks = rows // Q_ROWS
    nq = Q_ROWS * GRID_W

    def block(bq, carry):
        r = bq * Q_ROWS
        band0 = jnp.clip(r - WIN_H // 2, 0, rows - BAND_ROWS)
        case = jnp.where(bq == 0, 0, jnp.where(bq == n_blocks - 1, 2, 1))
        qs = pl.multiple_of(r * GRID_W, nq)
        ks = pl.multiple_of(band0 * GRID_W, (WIN_H // 2) * GRID_W)
        q = rms(q_ref[pl.ds(qs, nq), :], qg_ref[...]).astype(BF16)
        kb = kn_ref[pl.ds(ks, band), :]
        s_loc = lax.dot_general(q, kb, nt, preferred_element_type=F32) * (scale * LOG2E) + bias_ref[0, case]
        s_ctx = lax.dot_general(q, kc, nt, preferred_element_type=F32) * (scale * LOG2E)
        mx = jnp.maximum(jnp.max(s_loc, axis=-1, keepdims=True), jnp.max(s_ctx, axis=-1, keepdims=True))
        p_loc = jnp.exp2(s_loc - mx)
        p_ctx = jnp.exp2(s_ctx - mx)
        den = jnp.sum(p_loc, axis=-1, keepdims=True) + jnp.sum(p_ctx, axis=-1, keepdims=True)
        o = jnp.dot(p_loc.astype(BF16), vb_ref[pl.ds(ks, band), :], preferred_element_type=F32)
        o = o + jnp.dot(p_ctx.astype(BF16), vc, preferred_element_type=F32)
        o_ref[pl.ds(qs, nq), :] = (o / den).astype(o_ref.dtype)
        return carry

    lax.fori_loop(0, n_blocks, block, 0, unroll=2)


def _attention(z, q_g, k_g, bias_tab, n_lat, col0, width):
    m = z.shape[0]
    heads = width // HEAD_DIM
    rows = n_lat // GRID_W
    assert rows % Q_ROWS == 0 and rows >= 2 * BAND_ROWS and Q_ROWS % (WIN_H // 2) == 0
    cb = col0 // HEAD_DIM
    return pl.pallas_call(
        functools.partial(_attn_kernel, rows=rows),
        grid=(heads,),
        in_specs=[
            pl.BlockSpec((m, HEAD_DIM), lambda h: (0, cb + h)),
            pl.BlockSpec((m, HEAD_DIM), lambda h: (0, cb + heads + h)),
            pl.BlockSpec((m, HEAD_DIM), lambda h: (0, cb + 2 * heads + h)),
            pl.BlockSpec((1, HEAD_DIM), lambda h: (0, 0)),
            pl.BlockSpec((1, HEAD_DIM), lambda h: (0, 0)),
            pl.BlockSpec((1,) + bias_tab.shape[1:], lambda h: (h, 0, 0, 0)),
        ],
        out_specs=pl.BlockSpec((m, HEAD_DIM), lambda h: (0, h)),
        out_shape=jax.ShapeDtypeStruct((m, width), BF16),
        scratch_shapes=[pltpu.VMEM((m, HEAD_DIM), BF16), pltpu.VMEM((m, HEAD_DIM), BF16)],
        compiler_params=_cparams("parallel"),
        name="nbr_attention",
    )(z, z, z, q_g.reshape(1, HEAD_DIM), k_g.reshape(1, HEAD_DIM), bias_tab)


def _discretize_kernel(lr_ref, li_ref, ls_ref, br_ref, bi_ref, ar_ref, ai_ref, bbr_ref, bbi_ref):
    lr = lr_ref[0]
    li = li_ref[0]
    step = jnp.exp(ls_ref[0])
    mag = jnp.exp(lr * step)
    ang = li * step
    a_re = mag * jnp.cos(ang)
    a_im = mag * jnp.sin(ang)
    inv = 1.0 / (lr * lr + li * li)
    f_re = ((a_re - 1.0) * lr + a_im * li) * inv
    f_im = (a_im * lr - (a_re - 1.0) * li) * inv
    br = br_ref[0]
    bi = bi_ref[0]
    ar_ref[0] = a_re
    ai_ref[0] = a_im
    bbr_ref[0] = f_re * br - f_im * bi
    bbi_ref[0] = f_re * bi + f_im * br


def _discretize(lam_re, lam_im, log_step, b_re, b_im):
    n_dir, g, n = lam_re.shape
    p = b_re.shape[-1]
    rep = lambda a: jnp.repeat(a, p, axis=1)
    lr = rep(lam_re)
    li = rep(lam_im)
    ls = rep(jnp.broadcast_to(log_step[:, :, None], (n_dir, g, n)))
    br = b_re.transpose(0, 1, 3, 2).reshape(n_dir, g * p, n)
    bi = b_im.transpose(0, 1, 3, 2).reshape(n_dir, g * p, n)
    spec = pl.BlockSpec((1, g * p, n), lambda d: (d, 0, 0))
    shp = jax.ShapeDtypeStruct((n_dir, g * p, n), F32)
    return pl.pallas_call(
        _discretize_kernel,
        grid=(n_dir,),
        in_specs=[spec] * 5,
        out_specs=[spec] * 4,
        out_shape=[shp] * 4,
        compiler_params=_cparams("parallel"),
        name="s5_discretize",
    )(lr, li, ls, br, bi)


def _scan_weights(a_re, a_im, bbt_re, bbt_im, c_re, c_im):
    g, n = a_re.shape[0] // SSM_GROUP, a_re.shape[1]
    gb = LANES // SSM_GROUP
    nb = g // gb
    eye = jnp.eye(gb, dtype=F32)
    ar = a_re[::SSM_GROUP].reshape(g * n // LANES, LANES)
    ai = a_im[::SSM_GROUP].reshape(g * n // LANES, LANES)

    def in_mat(bbt):
        blk = bbt.reshape(nb, gb, SSM_GROUP, n)
        return jnp.einsum("bgpn,gh->bgphn", blk, eye).reshape(nb, LANES, gb * n)

    def out_mat(c):
        blk = c.reshape(nb, gb, SSM_GROUP, n)
        return jnp.einsum("bgpn,gh->bgnhp", blk, eye).reshape(nb, gb * n, LANES)

    wb = jnp.concatenate([in_mat(bbt_re), in_mat(bbt_im)], axis=2)
    wc = jnp.concatenate([out_mat(c_re), -out_mat(c_im)], axis=1)
    pieces = wb.shape[2] // PIECE
    wb = wb.reshape(nb, LANES, pieces, PIECE).transpose(0, 2, 1, 3).reshape(nb * pieces, LANES, PIECE)
    wc = wc.reshape(nb * pieces, PIECE, LANES)
    return ar, ai, wb.astype(BF16), wc.astype(BF16)


def _scan_kernel(u_ref, wb_ref, ar_ref, ai_ref, wc_ref, o_ref, xs0_ref, xs1_ref, xs2_ref, hs_ref, ub_ref,
                 y_ref, *, reverse):
    s = pl.program_id(0)
    t_len = u_ref.shape[0]
    n_piece = wb_ref.shape[0]
    nb = ub_ref.shape[0]
    ppb = n_piece // nb
    n_re = ar_ref.shape[0]
    spb = n_re // nb
    steps = t_len // n_piece
    im0 = n_re * SCAN_PITCH
    bufs = (xs0_ref, xs1_ref, xs2_ref)

    @pl.when(s == 0)
    def _():
        for ref in bufs:
            ref[...] = jnp.zeros_like(ref)
        hs_ref[...] = jnp.zeros_like(hs_ref)

    u = u_ref[...]
    for b in range(nb):
        ub_ref[b] = u[:, b * LANES:(b + 1) * LANES].astype(BF16)
    y_ref[...] = jnp.zeros_like(y_ref)

    def stages(nxt_ref, cur_ref, prv_ref):
        ar = ar_ref[...]
        ai = ai_ref[...]
        h_re = hs_ref[0]
        h_im = hs_ref[1]
        for j in range(n_piece):
            b = j // ppb
            q = j % ppb
            row0 = ((q // 2) * n_re + b * spb + (q % 2) * 2) * SCAN_PITCH
            hp = jnp.concatenate(
                [prv_ref[pl.ds(row0, t_len), :], prv_ref[pl.ds(row0 + SCAN_PITCH, t_len), :]], axis=1)
            y_ref[b] += jnp.dot(hp.astype(BF16), wc_ref[j], preferred_element_type=F32)
            xb = jnp.dot(ub_ref[b], wb_ref[j], preferred_element_type=F32)
            nxt_ref[pl.ds(row0, t_len), :] = xb[:, :LANES]
            nxt_ref[pl.ds(row0 + SCAN_PITCH, t_len), :] = xb[:, LANES:]
            for jj in range(j * steps, (j + 1) * steps):
                t = (t_len - 1 - jj) if reverse else jj
                x_re = cur_ref[pl.ds(t, n_re, stride=SCAN_PITCH), :]
                x_im = cur_ref[pl.ds(im0 + t, n_re, stride=SCAN_PITCH), :]
                h_re, h_im = ar * h_re - ai * h_im + x_re, ar * h_im + ai * h_re + x_im
                cur_ref[pl.ds(t, n_re, stride=SCAN_PITCH), :] = h_re
                cur_ref[pl.ds(im0 + t, n_re, stride=SCAN_PITCH), :] = h_im
        hs_ref[0] = h_re
        hs_ref[1] = h_im

    phase = lax.rem(s, 3)
    for r in range(3):
        pl.when(phase == r)(functools.partial(stages, bufs[r], bufs[(r + 2) % 3], bufs[(r + 1) % 3]))

    o_ref[...] = jnp.concatenate([y_ref[b] for b in range(nb)], axis=1)


def _s5_out_kernel(yf_ref, yb_ref, u_ref, d_ref, gw_ref, gb_ref, o_ref, gwbf_ref):
    @pl.when(pl.program_id(0) == 0)
    def _():
        gwbf_ref[...] = gw_ref[...].astype(BF16)

    y = yf_ref[...] + yb_ref[...] + u_ref[...] * d_ref[...]
    g = jax.nn.gelu(y)
    gate = jnp.dot(g.astype(BF16), gwbf_ref[...], preferred_element_type=F32) + gb_ref[...]
    o_ref[...] = (g * _sigmoid(gate)).astype(o_ref.dtype)


def _s5_mixer(z, col0, ch, scan_w, d_skip, glu_w, layer, glu_b, n_lat):
    m = z.shape[0]
    t_len = SEQ_TILE
    assert m - n_lat == t_len
    n_chunks = m // t_len
    ctx_chunk = n_chunks - 1
    cb = col0 // ch
    ar, ai, wb, wc = scan_w
    nb = ch // LANES
    n_re = ar.shape[1]
    assert t_len % wb.shape[1] == 0

    def scan(direction, order, reverse, name):
        which = 2 * layer + direction

        def const_spec(a):
            nd = a.ndim - 1
            return pl.BlockSpec((None,) + a.shape[1:], lambda s: (which,) + (0,) * nd)

        first = lambda s: order(jnp.minimum(s, n_chunks - 1))
        third = lambda s: order(jnp.clip(s - 2, 0, n_chunks - 1))
        return pl.pallas_call(
            functools.partial(_scan_kernel, reverse=reverse),
            grid=(n_chunks + 2,),
            in_specs=[pl.BlockSpec((t_len, ch), lambda s: (first(s), cb))]
            + [const_spec(a) for a in (wb, ar, ai, wc)],
            out_specs=pl.BlockSpec((t_len, ch), lambda s: (third(s), 0)),
            out_shape=jax.ShapeDtypeStruct((m, ch), F32),
            scratch_shapes=[pltpu.VMEM((2 * n_re * SCAN_PITCH, LANES), F32)] * 3 + [
                pltpu.VMEM((2, n_re, LANES), F32),
                pltpu.VMEM((nb, t_len, LANES), BF16),
                pltpu.VMEM((nb, t_len, LANES), F32),
            ],
            compiler_params=_cparams("arbitrary"),
            name=name,
        )(z, wb, ar, ai, wc)

    y_f = scan(0, lambda k: jnp.where(k == 0, ctx_chunk, k - 1), False, "s5_scan_fwd")
    y_b = scan(1, lambda k: jnp.where(k == 0, ctx_chunk, ctx_chunk - k), True, "s5_scan_bwd")

    tm = m // 8
    return pl.pallas_call(
        _s5_out_kernel,
        grid=(m // tm,),
        in_specs=[
            pl.BlockSpec((tm, ch), lambda i: (i, 0)),
            pl.BlockSpec((tm, ch), lambda i: (i, 0)),
            pl.BlockSpec((tm, ch), lambda i: (i, cb)),
            pl.BlockSpec((1, ch), lambda i: (0, 0)),
            pl.BlockSpec((None, ch, ch), lambda i: (layer, 0, 0)),
            pl.BlockSpec((1, ch), lambda i: (0, 0)),
        ],
        out_specs=pl.BlockSpec((tm, ch), lambda i: (i, 0)),
        out_shape=jax.ShapeDtypeStruct((m, ch), BF16),
        scratch_shapes=[pltpu.VMEM((ch, ch), BF16)],
        compiler_params=_cparams("arbitrary"),
        name="s5_out",
    )(y_f, y_b, z, d_skip.reshape(1, ch), glu_w, glu_b.reshape(1, ch))


def _layer(xa, cc, n_lat, layer, last, w_mod, b_mod, g_norm1, g_norm2, w_in, conv_w, conv_b, conv_ln_g,
           conv_ln_b, conv_pw_w, conv_pw_b, q_norm_g, k_norm_g, rpb, scan_w, ssm_d, glu_w, glu_b, w_out,
           w_ff1, w_ff3, w_ff2):
    unstacked = isinstance(xa, tuple)
    m_all = sum(a.shape[0] for a in xa) if unstacked else xa.shape[0]
    d = w_in.shape[1]
    conv_ch = conv_w.shape[1]
    ssm_ch = ssm_d.shape[0]
    na_width = (w_in.shape[2] - 2 * conv_ch - ssm_ch) // 3
    o1 = 2 * conv_ch
    o4 = o1 + 3 * na_width
    m_out = n_lat if last else m_all
    tm_all = m_all // 8
    tm_out = m_out // 8

    mod = _modulation(cc, w_mod, layer, b_mod)[:2]
    sh1, sc1, g1, sh2, sc2, g2 = [mod[:, i * d:(i + 1) * d] for i in range(6)]

    if unstacked:
        a, xa = _stack_norm_mod(xa[0], xa[1], g_norm1, sh1, sc1)
    else:
        a = _norm_mod(xa, g_norm1, sh1, sc1, n_lat, m_all)
    z = _proj(a, w_in, layer, F32, tm_all, 512)

    y_conv = _conv_module(z, conv_w, conv_b, conv_ln_g, conv_ln_b, conv_pw_w, layer, conv_pw_b, n_lat)
    y_na = _attention(z, q_norm_g, k_norm_g, _bias_table(rpb), n_lat, o1, na_width)
    y_s5 = _s5_mixer(z, o4, ssm_ch, scan_w, ssm_d, glu_w, layer, glu_b, n_lat)

    xa = _mix_out(y_conv, y_na, y_s5, w_out, layer, xa, g1, n_lat, m_out, tm_out, 512)
    hidden = _ffn_up(_norm_mod(xa, g_norm2, sh2, sc2, n_lat, m_out), w_ff1, w_ff3, layer, tm_out, 256)
    return _ffn_down(hidden, w_ff2, layer, xa, g2, n_lat, tm_out // 2, 512)


def kernel(x, c, ctx, c_ctx, w_mod, b_mod, g_norm1, g_norm2, w_in, conv_w, conv_b, conv_ln_g, conv_ln_b,
           conv_pw_w, conv_pw_b, q_norm_g, k_norm_g, rpb, ssm_lambda_re, ssm_lambda_im, ssm_log_step,
           ssm_b_re, ssm_b_im, ssm_c_re, ssm_c_im, ssm_d, ssm_glu_w, ssm_glu_b, w_out, w_ff1, w_ff3, w_ff2):
    batch, seq, d = x.shape
    n_ctx = ctx.shape[1]
    assert batch == 1 and n_ctx == SEQ_TILE and seq % GRID_W == 0
    depth = w_mod.shape[0]
    xa = (x[0], ctx[0])
    cc = jnp.zeros((SUBLANES, d), F32).at[0].set(c[0]).at[1].set(c_ctx)
    w_ff2 = w_ff2.astype(BF16)
    flat = lambda a: a.reshape((-1,) + a.shape[2:])
    disc = _discretize(flat(ssm_lambda_re), flat(ssm_lambda_im), flat(ssm_log_step), flat(ssm_b_re),
                       flat(ssm_b_im))
    scan_w = jax.vmap(_scan_weights)(*disc, flat(ssm_c_re), flat(ssm_c_im))
    for l in range(depth):
        xa = _layer(xa, cc, seq, l, l == depth - 1, w_mod, b_mod[l], g_norm1[l], g_norm2[l], w_in, conv_w[l],
                    conv_b[l], conv_ln_g[l], conv_ln_b[l], conv_pw_w, conv_pw_b[l], q_norm_g[l], k_norm_g[l],
                    rpb[l], scan_w, ssm_d[l], ssm_glu_w, ssm_glu_b[l], w_out, w_ff1, w_ff3, w_ff2)
    return xa[None]
```

```python
import functools

import jax
import jax.numpy as jnp
from jax import lax
from jax.experimental import pallas as pl
from jax.experimental.pallas import tpu as pltpu

F32 = jnp.float32
BF16 = jnp.bfloat16

GRID_W = 64
WIN_H = 8
WIN_W = 16
HEAD_DIM = 128
CONV_K = 31
SSM_GROUP = 16
SSM_STATE = 64
EPS = 1e-6
NEG_INF = -1e30
LOG2E = 1.4426950408889634

LANES = 128
SUBLANES = 8
VMEM_LIMIT = 56 * 1024 * 1024

SEQ_TILE = 256
SCAN_PITCH = SEQ_TILE + SUBLANES
PIECE = 2 * LANES
HALO = 16


def _cparams(*sem):
    return pltpu.CompilerParams(dimension_semantics=sem, vmem_limit_bytes=VMEM_LIMIT)


def _sigmoid(x):
    return 1.0 / (1.0 + jnp.exp(-x))


def _silu(x):
    return x * _sigmoid(x)


def _mod_kernel(c_ref, w_ref, b_ref, o_ref):
    s = _silu(c_ref[...]).astype(BF16)
    o_ref[...] = jnp.dot(s, w_ref[...].astype(BF16), preferred_element_type=F32) + b_ref[...]


def _modulation(cc, w_mod, layer, b_mod):
    _, d, n = w_mod.shape
    tn = 512
    return pl.pallas_call(
        _mod_kernel,
        grid=(n // tn,),
        in_specs=[
            pl.BlockSpec((SUBLANES, d), lambda j: (0, 0)),
            pl.BlockSpec((None, d, tn), lambda j: (layer, 0, j)),
            pl.BlockSpec((1, tn), lambda j: (0, j)),
        ],
        out_specs=pl.BlockSpec((SUBLANES, tn), lambda j: (0, j)),
        out_shape=jax.ShapeDtypeStruct((SUBLANES, n), F32),
        compiler_params=_cparams("parallel"),
        name="adaln_mod",
    )(cc, w_mod, b_mod.reshape(1, n))


def _norm_mod_kernel(x_ref, g_ref, sh_ref, sc_ref, o_ref, *, n_lat_tiles):
    i = pl.program_id(0)
    x = x_ref[...]
    y = x * lax.rsqrt(jnp.mean(x * x, axis=-1, keepdims=True) + EPS) * g_ref[...]
    row = jnp.where(i >= n_lat_tiles, 1, 0)
    sh = sh_ref[pl.ds(row, 1), :]
    sc = sc_ref[pl.ds(row, 1), :]
    o_ref[...] = (y * (1.0 + sc) + sh).astype(BF16)


def _stack_norm_mod_kernel(lat_ref, ctx_ref, g_ref, sh_ref, sc_ref, o_ref, xa_ref, *, n_lat_tiles):
    i = pl.program_id(0)

    def emit(x, row):
        xa_ref[...] = x
        y = x * lax.rsqrt(jnp.mean(x * x, axis=-1, keepdims=True) + EPS) * g_ref[...]
        o_ref[...] = (y * (1.0 + sc_ref[row:row + 1, :]) + sh_ref[row:row + 1, :]).astype(BF16)

    @pl.when(i < n_lat_tiles)
    def _():
        emit(lat_ref[...], 0)

    @pl.when(i >= n_lat_tiles)
    def _():
        emit(ctx_ref[...], 1)


def _stack_norm_mod(lat, ctx, g, sh, sc):
    n_lat, d = lat.shape
    tm = SEQ_TILE
    n_lat_tiles = n_lat // tm
    m = n_lat + ctx.shape[0]
    return pl.pallas_call(
        functools.partial(_stack_norm_mod_kernel, n_lat_tiles=n_lat_tiles),
        grid=(m // tm,),
        in_specs=[
            pl.BlockSpec((tm, d), lambda i: (jnp.minimum(i, n_lat_tiles - 1), 0)),
            pl.BlockSpec((tm, d), lambda i: (jnp.maximum(i - n_lat_tiles, 0), 0)),
            pl.BlockSpec((1, d), lambda i: (0, 0)),
            pl.BlockSpec((2, d), lambda i: (0, 0)),
            pl.BlockSpec((2, d), lambda i: (0, 0)),
        ],
        out_specs=[pl.BlockSpec((tm, d), lambda i: (i, 0)), pl.BlockSpec((tm, d), lambda i: (i, 0))],
        out_shape=[jax.ShapeDtypeStruct((m, d), BF16), jax.ShapeDtypeStruct((m, d), F32)],
        compiler_params=_cparams("parallel"),
        name="stack_norm_mod",
    )(lat, ctx, g.reshape(1, d), sh, sc)


def _norm_mod(x, g, sh, sc, n_lat, m):
    d = x.shape[1]
    tm = SEQ_TILE
    return pl.pallas_call(
        functools.partial(_norm_mod_kernel, n_lat_tiles=n_lat // tm),
        grid=(m // tm,),
        in_specs=[
            pl.BlockSpec((tm, d), lambda i: (i, 0)),
            pl.BlockSpec((1, d), lambda i: (0, 0)),
            pl.BlockSpec((2, d), lambda i: (0, 0)),
            pl.BlockSpec((2, d), lambda i: (0, 0)),
        ],
        out_specs=pl.BlockSpec((tm, d), lambda i: (i, 0)),
        out_shape=jax.ShapeDtypeStruct((m, d), BF16),
        compiler_params=_cparams("parallel"),
        name="norm_mod",
    )(x, g.reshape(1, d), sh, sc)


def _proj_kernel(a_ref, w_ref, o_ref):
    o_ref[...] = jnp.dot(a_ref[...], w_ref[...].astype(BF16),
                         preferred_element_type=F32).astype(o_ref.dtype)


def _proj(a, w, layer, out_dtype, tm, tn):
    m, k = a.shape
    n = w.shape[2]
    return pl.pallas_call(
        _proj_kernel,
        grid=(m // tm, n // tn),
        in_specs=[
            pl.BlockSpec((tm, k), lambda i, j: (i, 0)),
            pl.BlockSpec((None, k, tn), lambda i, j: (layer, 0, j)),
        ],
        out_specs=pl.BlockSpec((tm, tn), lambda i, j: (i, j)),
        out_shape=jax.ShapeDtypeStruct((m, n), out_dtype),
        compiler_params=_cparams("parallel", "arbitrary"),
        name="proj",
    )(a, w)


def _gate_rows(gate_ref, tm, n_lat):
    rows = pl.program_id(0) * tm + lax.broadcasted_iota(jnp.int32, (tm, 1), 0)
    return jnp.where(rows < n_lat, gate_ref[0:1, :], gate_ref[1:2, :])


def _mix_out_kernel(a0_ref, a1_ref, a2_ref, w_ref, res_ref, gate_ref, o_ref, *, tm, n_lat):
    a = jnp.concatenate([a0_ref[...], a1_ref[...], a2_ref[...]], axis=1)
    acc = jnp.dot(a, w_ref[...].astype(BF16), preferred_element_type=F32)
    o_ref[...] = res_ref[...] + _gate_rows(gate_ref, tm, n_lat) * acc


def _mix_out(a0, a1, a2, w, layer, res, gate, n_lat, m, tm, tn):
    _, k, n = w.shape
    return pl.pallas_call(
        functools.partial(_mix_out_kernel, tm=tm, n_lat=n_lat),
        grid=(m // tm, n // tn),
        in_specs=[
            pl.BlockSpec((tm, a0.shape[1]), lambda i, j: (i, 0)),
            pl.BlockSpec((tm, a1.shape[1]), lambda i, j: (i, 0)),
            pl.BlockSpec((tm, a2.shape[1]), lambda i, j: (i, 0)),
            pl.BlockSpec((None, k, tn), lambda i, j: (layer, 0, j)),
            pl.BlockSpec((tm, tn), lambda i, j: (i, j)),
            pl.BlockSpec((2, tn), lambda i, j: (0, j)),
        ],
        out_specs=pl.BlockSpec((tm, tn), lambda i, j: (i, j)),
        out_shape=jax.ShapeDtypeStruct((m, n), F32),
        compiler_params=_cparams("parallel", "arbitrary"),
        name="mix_out",
    )(a0, a1, a2, w, res, gate)


def _ffn_up_kernel(a_ref, w1_ref, w3_ref, w2_ref, o_ref, w2b_ref):
    a = a_ref[...]
    t = jnp.dot(a, w1_ref[...].astype(BF16), preferred_element_type=F32)
    u = jnp.dot(a, w3_ref[...].astype(BF16), preferred_element_type=F32)
    o_ref[...] = (_silu(t) * u).astype(o_ref.dtype)
    w2b_ref[...] = w2_ref[...].astype(BF16)


def _ffn_up(a, w1, w3, w2, layer, tm, tn):
    m, k = a.shape
    n = w1.shape[2]
    n2 = w2.shape[2]
    ni, nj = m // tm, n // tn
    rows = n // (ni * nj)
    assert rows * ni * nj == n and rows % (2 * SUBLANES) == 0
    return pl.pallas_call(
        _ffn_up_kernel,
        grid=(ni, nj),
        in_specs=[
            pl.BlockSpec((tm, k), lambda i, j: (i, 0)),
            pl.BlockSpec((None, k, tn), lambda i, j: (layer, 0, j)),
            pl.BlockSpec((None, k, tn), lambda i, j: (layer, 0, j)),
            pl.BlockSpec((None, rows, n2), lambda i, j: (layer, i * nj + j, 0)),
        ],
        out_specs=[
            pl.BlockSpec((tm, tn), lambda i, j: (i, j)),
            pl.BlockSpec((rows, n2), lambda i, j: (i * nj + j, 0)),
        ],
        out_shape=[jax.ShapeDtypeStruct((m, n), BF16), jax.ShapeDtypeStruct((n, n2), BF16)],
        compiler_params=_cparams("parallel", "arbitrary"),
        name="ffn_up",
    )(a, w1, w3, w2)


def _ffn_down_kernel(a_ref, w_ref, res_ref, gate_ref, o_ref, *, tm, n_lat):
    acc = jnp.dot(a_ref[...], w_ref[...], preferred_element_type=F32)
    o_ref[...] = res_ref[...] + _gate_rows(gate_ref, tm, n_lat) * acc


def _ffn_down(a, w, res, gate, n_lat, tm, tn):
    m, k = a.shape
    n = w.shape[1]
    return pl.pallas_call(
        functools.partial(_ffn_down_kernel, tm=tm, n_lat=n_lat),
        grid=(m // tm, n // tn),
        in_specs=[
            pl.BlockSpec((tm, k), lambda i, j: (i, 0)),
            pl.BlockSpec((k, tn), lambda i, j: (0, j)),
            pl.BlockSpec((tm, tn), lambda i, j: (i, j)),
            pl.BlockSpec((2, tn), lambda i, j: (0, j)),
        ],
        out_specs=pl.BlockSpec((tm, tn), lambda i, j: (i, j)),
        out_shape=jax.ShapeDtypeStruct((m, n), F32),
        compiler_params=_cparams("parallel", "arbitrary"),
        name="ffn_down",
    )(a, w, res, gate)


def _conv_kernel(main_ref, prev_ref, next_ref, cw_ref, cb_ref, lng_ref, lnb_ref, pw_ref, pwb_ref,
                 o_ref, uext_ref, yc_ref, wbf_ref, *, n_lat_tiles, n_tiles):
    i = pl.program_id(0)
    tl = main_ref.shape[0]
    ch = o_ref.shape[1]
    nblk = ch // LANES

    @pl.when(i == 0)
    def _():
        wbf_ref[...] = pw_ref[...].astype(BF16)

    def glu(a):
        return a[:, :ch] * _sigmoid(a[:, ch:])

    is_start = jnp.logical_or(i == 0, i == n_lat_tiles)
    is_end = jnp.logical_or(i == n_lat_tiles - 1, i == n_tiles - 1)
    u = glu(main_ref[...])
    up = jnp.where(is_start, 0.0, glu(prev_ref[...]))
    un = jnp.where(is_end, 0.0, glu(next_ref[...]))
    for c in range(nblk):
        sl = slice(c * LANES, (c + 1) * LANES)
        uext_ref[c, 0:HALO, :] = up[:, sl]
        uext_ref[c, HALO:HALO + tl, :] = u[:, sl]
        uext_ref[c, HALO + tl:2 * HALO + tl, :] = un[:, sl]

    def chan_block(c, carry):
        acc = jnp.zeros((tl, LANES), F32)
        for k in range(CONV_K):
            acc = acc + uext_ref[c, pl.ds(k + HALO - CONV_K // 2, tl), :] * cw_ref[c, k:k + 1, :]
        yc_ref[c] = acc + cb_ref[c]
        return carry

    lax.fori_loop(0, nblk, chan_block, 0)

    y = jnp.concatenate([yc_ref[c] for c in range(nblk)], axis=1)
    mu = jnp.mean(y, axis=-1, keepdims=True)
    yd = y - mu
    var = jnp.mean(yd * yd, axis=-1, keepdims=True)
    yn = yd * lax.rsqrt(var + EPS) * lng_ref[...] + lnb_ref[...]
    s = _silu(yn).astype(BF16)
    o_ref[...] = (jnp.dot(s, wbf_ref[...], preferred_element_type=F32) + pwb_ref[...]).astype(o_ref.dtype)


def _conv_module(z, conv_w, conv_b, ln_g, ln_b, pw_w, layer, pw_b, n_lat):
    m = z.shape[0]
    ch = conv_w.shape[1]
    tl = SEQ_TILE
    nblk = ch // LANES
    n_tiles = m // tl
    hb = tl // HALO
    cw = conv_w.reshape(CONV_K, nblk, LANES).transpose(1, 0, 2)
    cb = conv_b.reshape(nblk, 1, LANES)
    return pl.pallas_call(
        functools.partial(_conv_kernel, n_lat_tiles=n_lat // tl, n_tiles=n_tiles),
        grid=(n_tiles,),
        in_specs=[
            pl.BlockSpec((tl, 2 * ch), lambda i: (i, 0)),
            pl.BlockSpec((HALO, 2 * ch), lambda i: (jnp.maximum(i * hb - 1, 0), 0)),
            pl.BlockSpec((HALO, 2 * ch), lambda i: (jnp.minimum((i + 1) * hb, n_tiles * hb - 1), 0)),
            pl.BlockSpec((nblk, CONV_K, LANES), lambda i: (0, 0, 0)),
            pl.BlockSpec((nblk, 1, LANES), lambda i: (0, 0, 0)),
            pl.BlockSpec((1, ch), lambda i: (0, 0)),
            pl.BlockSpec((1, ch), lambda i: (0, 0)),
            pl.BlockSpec((None, ch, ch), lambda i: (layer, 0, 0)),
            pl.BlockSpec((1, ch), lambda i: (0, 0)),
        ],
        out_specs=pl.BlockSpec((tl, ch), lambda i: (i, 0)),
        out_shape=jax.ShapeDtypeStruct((m, ch), BF16),
        scratch_shapes=[
            pltpu.VMEM((nblk, tl + 2 * HALO, LANES), F32),
            pltpu.VMEM((nblk, tl, LANES), F32),
            pltpu.VMEM((ch, ch), BF16),
        ],
        compiler_params=_cparams("arbitrary"),
        name="conv_module",
    )(z, z, z, cw, cb, ln_g.reshape(1, ch), ln_b.reshape(1, ch), pw_w, pw_b.reshape(1, ch))


Q_ROWS = 4
BAND_ROWS = Q_ROWS + WIN_H
_BLOCK_CASES = (
    (0, lambda a: max(a - WIN_H // 2, 0)),
    (-WIN_H // 2, lambda a: a),
    (-WIN_H, lambda a: min(a + WIN_H // 2, Q_ROWS)),
)


def _bias_table_kernel(rpb_ref, o_ref):
    h = pl.program_id(0)
    qc = lax.broadcasted_iota(jnp.int32, (GRID_W, GRID_W), 0)
    kc = lax.broadcasted_iota(jnp.int32, (GRID_W, GRID_W), 1)
    qstart = jnp.clip(qc - WIN_W // 2, 0, GRID_W - WIN_W)
    valid = jnp.logical_and(kc >= qstart, kc < qstart + WIN_W)
    dc = jnp.where(valid, kc - qc + WIN_W - 1, -1)
    tabs = []
    for dr in range(2 * WIN_H - 1):
        t = jnp.full((GRID_W, GRID_W), NEG_INF, F32)
        for d in range(2 * WIN_W - 1):
            t = jnp.where(dc == d, rpb_ref[h, dr, d] * LOG2E, t)
        tabs.append(t)
    masked = jnp.full((GRID_W, GRID_W), NEG_INF, F32)
    for case, (band0, first_valid) in enumerate(_BLOCK_CASES):
        for a in range(Q_ROWS):
            blocks = []
            for i in range(BAND_ROWS):
                in_window = first_valid(a) <= i < first_valid(a) + WIN_H
                dr = band0 + i - a + WIN_H - 1
                blocks.append(tabs[dr] if in_window else masked)
            o_ref[0, case, a * GRID_W:(a + 1) * GRID_W, :] = jnp.concatenate(blocks, axis=1)


def _bias_table(rpb):
    heads = rpb.shape[0]
    shape = (len(_BLOCK_CASES), Q_ROWS * GRID_W, BAND_ROWS * GRID_W)
    return pl.pallas_call(
        _bias_table_kernel,
        grid=(heads,),
        in_specs=[pl.BlockSpec(memory_space=pltpu.SMEM)],
        out_specs=pl.BlockSpec((1,) + shape, lambda h: (h, 0, 0, 0)),
        out_shape=jax.ShapeDtypeStruct((heads,) + shape, F32),
        compiler_params=_cparams("parallel"),
        name="bias_table",
    )(rpb)


def _attn_kernel(q_ref, k_ref, v_ref, qg_ref, kg_ref, bias_ref, o_ref, kn_ref, vb_ref, *, rows):
    m = q_ref.shape[0]
    scale = HEAD_DIM ** -0.5
    band = BAND_ROWS * GRID_W
    nt = (((1,), (1,)), ((), ()))

    def rms(x, g):
        return x * lax.rsqrt(jnp.mean(x * x, axis=-1, keepdims=True) + EPS) * g

    def prep(c, carry):
        r = pl.multiple_of(c * SEQ_TILE, SEQ_TILE)
        kn_ref[pl.ds(r, SEQ_TILE), :] = rms(k_ref[pl.ds(r, SEQ_TILE), :], kg_ref[...]).astype(BF16)
        vb_ref[pl.ds(r, SEQ_TILE), :] = v_ref[pl.ds(r, SEQ_TILE), :].astype(BF16)
        return carry

    lax.fori_loop(0, m // SEQ_TILE, prep, 0, unroll=3)

    n_lat = rows * GRID_W
    kc = kn_ref[n_lat:m, :]
    vc = vb_ref[n_lat:m, :]

    qc = rms(q_ref[n_lat:m, :], qg_ref[...]).astype(BF16)
    s = lax.dot_general(qc, kc, nt, preferred_element_type=F32) * scale
    p = jnp.exp(s - jnp.max(s, axis=-1, keepdims=True))
    l = jnp.sum(p, axis=-1, keepdims=True)
    o_ref[n_lat:m, :] = (jnp.dot(p.astype(BF16), vc, preferred_element_type=F32) / l).astype(o_ref.dtype)

    n_blocks = rows // Q_ROWS
    nq = Q_ROWS * GRID_W

    def block(bq, carry):
        r = bq * Q_ROWS
        band0 = jnp.clip(r - WIN_H // 2, 0, rows - BAND_ROWS)
        case = jnp.where(bq == 0, 0, jnp.where(bq == n_blocks - 1, 2, 1))
        qs = pl.multiple_of(r * GRID_W, nq)
        ks = pl.multiple_of(band0 * GRID_W, (WIN_H // 2) * GRID_W)
        q = rms(q_ref[pl.ds(qs, nq), :], qg_ref[...]).astype(BF16)
        kb = kn_ref[pl.ds(ks, band), :]
        s_loc = lax.dot_general(q, kb, nt, preferred_element_type=F32) * (scale * LOG2E) + bias_ref[0, case]
        s_ctx = lax.dot_general(q, kc, nt, preferred_element_type=F32) * (scale * LOG2E)
        mx = jnp.maximum(jnp.max(s_loc, axis=-1, keepdims=True), jnp.max(s_ctx, axis=-1, keepdims=True))
        p_loc = jnp.exp2(s_loc - mx)
        p_ctx = jnp.exp2(s_ctx - mx)
        den = jnp.sum(p_loc, axis=-1, keepdims=True) + jnp.sum(p_ctx, axis=-1, keepdims=True)
        o = jnp.dot(p_loc.astype(BF16), vb_ref[pl.ds(ks, band), :], preferred_element_type=F32)
        o = o + jnp.dot(p_ctx.astype(BF16), vc, preferred_element_type=F32)
        o_ref[pl.ds(qs, nq), :] = (o / den).astype(o_ref.dtype)
        return carry

    lax.fori_loop(0, n_blocks, block, 0, unroll=2)


def _attention(z, q_g, k_g, bias_tab, n_lat, col0, width):
    m = z.shape[0]
    heads = width // HEAD_DIM
    rows = n_lat // GRID_W
    assert rows % Q_ROWS == 0 and rows >= 2 * BAND_ROWS and Q_ROWS % (WIN_H // 2) == 0
    cb = col0 // HEAD_DIM
    return pl.pallas_call(
        functools.partial(_attn_kernel, rows=rows),
        grid=(heads,),
        in_specs=[
            pl.BlockSpec((m, HEAD_DIM), lambda h: (0, cb + h)),
            pl.BlockSpec((m, HEAD_DIM), lambda h: (0, cb + heads + h)),
            pl.BlockSpec((m, HEAD_DIM), lambda h: (0, cb + 2 * heads + h)),
            pl.BlockSpec((1, HEAD_DIM), lambda h: (0, 0)),
            pl.BlockSpec((1, HEAD_DIM), lambda h: (0, 0)),
            pl.BlockSpec((1,) + bias_tab.shape[1:], lambda h: (h, 0, 0, 0)),
        ],
        out_specs=pl.BlockSpec((m, HEAD_DIM), lambda h: (0, h)),
        out_shape=jax.ShapeDtypeStruct((m, width), BF16),
        scratch_shapes=[pltpu.VMEM((m, HEAD_DIM), BF16), pltpu.VMEM((m, HEAD_DIM), BF16)],
        compiler_params=_cparams("parallel"),
        name="nbr_attention",
    )(z, z, z, q_g.reshape(1, HEAD_DIM), k_g.reshape(1, HEAD_DIM), bias_tab)


def _discretize_kernel(lr_ref, li_ref, ls_ref, br_ref, bi_ref, ar_ref, ai_ref, bbr_ref, bbi_ref):
    lr = lr_ref[0]
    li = li_ref[0]
    step = jnp.exp(ls_ref[0])
    mag = jnp.exp(lr * step)
    ang = li * step
    a_re = mag * jnp.cos(ang)
    a_im = mag * jnp.sin(ang)
    inv = 1.0 / (lr * lr + li * li)
    f_re = ((a_re - 1.0) * lr + a_im * li) * inv
    f_im = (a_im * lr - (a_re - 1.0) * li) * inv
    br = br_ref[0]
    bi = bi_ref[0]
    ar_ref[0] = a_re
    ai_ref[0] = a_im
    bbr_ref[0] = f_re * br - f_im * bi
    bbi_ref[0] = f_re * bi + f_im * br


def _discretize(lam_re, lam_im, log_step, b_re, b_im):
    n_dir, g, n = lam_re.shape
    p = b_re.shape[-1]
    rep = lambda a: jnp.repeat(a, p, axis=1)
    lr = rep(lam_re)
    li = rep(lam_im)
    ls = rep(jnp.broadcast_to(log_step[:, :, None], (n_dir, g, n)))
    br = b_re.transpose(0, 1, 3, 2).reshape(n_dir, g * p, n)
    bi = b_im.transpose(0, 1, 3, 2).reshape(n_dir, g * p, n)
    spec = pl.BlockSpec((1, g * p, n), lambda d: (d, 0, 0))
    shp = jax.ShapeDtypeStruct((n_dir, g * p, n), F32)
    return pl.pallas_call(
        _discretize_kernel,
        grid=(n_dir,),
        in_specs=[spec] * 5,
        out_specs=[spec] * 4,
        out_shape=[shp] * 4,
        compiler_params=_cparams("parallel"),
        name="s5_discretize",
    )(lr, li, ls, br, bi)


def _scan_weights(a_re, a_im, bbt_re, bbt_im, c_re, c_im):
    g, n = a_re.shape[0] // SSM_GROUP, a_re.shape[1]
    gb = LANES // SSM_GROUP
    nb = g // gb
    eye = jnp.eye(gb, dtype=F32)
    ar = a_re[::SSM_GROUP].reshape(g * n // LANES, LANES)
    ai = a_im[::SSM_GROUP].reshape(g * n // LANES, LANES)

    def in_mat(bbt):
        blk = bbt.reshape(nb, gb, SSM_GROUP, n)
        return jnp.einsum("bgpn,gh->bgphn", blk, eye).reshape(nb, LANES, gb * n)

    def out_mat(c):
        blk = c.reshape(nb, gb, SSM_GROUP, n)
        return jnp.einsum("bgpn,gh->bgnhp", blk, eye).reshape(nb, gb * n, LANES)

    wb = jnp.concatenate([in_mat(bbt_re), in_mat(bbt_im)], axis=2)
    wc = jnp.concatenate([out_mat(c_re), -out_mat(c_im)], axis=1)
    pieces = wb.shape[2] // PIECE
    wb = wb.reshape(nb, LANES, pieces, PIECE).transpose(0, 2, 1, 3).reshape(nb * pieces, LANES, PIECE)
    wc = wc.reshape(nb * pieces, PIECE, LANES)
    return ar, ai, wb.astype(BF16), wc.astype(BF16)


def _scan_kernel(u_ref, wb_ref, ar_ref, ai_ref, wc_ref, o_ref, xs0_ref, xs1_ref, xs2_ref, hs_ref, ub_ref,
                 y_ref, *, reverse):
    s = pl.program_id(0)
    t_len = u_ref.shape[0]
    n_piece = wb_ref.shape[0]
    nb = ub_ref.shape[0]
    ppb = n_piece // nb
    n_re = ar_ref.shape[0]
    spb = n_re // nb
    steps = t_len // n_piece
    im0 = n_re * SCAN_PITCH
    bufs = (xs0_ref, xs1_ref, xs2_ref)

    @pl.when(s == 0)
    def _():
        for ref in bufs:
            ref[...] = jnp.zeros_like(ref)
        hs_ref[...] = jnp.zeros_like(hs_ref)

    u = u_ref[...]
    for b in range(nb):
        ub_ref[b] = u[:, b * LANES:(b + 1) * LANES].astype(BF16)
    y_ref[...] = jnp.zeros_like(y_ref)

    def stages(nxt_ref, cur_ref, prv_ref):
        ar = ar_ref[...]
        ai = ai_ref[...]
        h_re = hs_ref[0]
        h_im = hs_ref[1]
        for j in range(n_piece):
            b = j // ppb
            q = j % ppb
            row0 = ((q // 2) * n_re + b * spb + (q % 2) * 2) * SCAN_PITCH
            hp = jnp.concatenate(
                [prv_ref[pl.ds(row0, t_len), :], prv_ref[pl.ds(row0 + SCAN_PITCH, t_len), :]], axis=1)
            y_ref[b] += jnp.dot(hp.astype(BF16), wc_ref[j], preferred_element_type=F32)
            xb = jnp.dot(ub_ref[b], wb_ref[j], preferred_element_type=F32)
            nxt_ref[pl.ds(row0, t_len), :] = xb[:, :LANES]
            nxt_ref[pl.ds(row0 + SCAN_PITCH, t_len), :] = xb[:, LANES:]
            for jj in range(j * steps, (j + 1) * steps):
                t = (t_len - 1 - jj) if reverse else jj
                x_re = cur_ref[pl.ds(t, n_re, stride=SCAN_PITCH), :]
                x_im = cur_ref[pl.ds(im0 + t, n_re, stride=SCAN_PITCH), :]
                h_re, h_im = ar * h_re - ai * h_im + x_re, ar * h_im + ai * h_re + x_im
                cur_ref[pl.ds(t, n_re, stride=SCAN_PITCH), :] = h_re
                cur_ref[pl.ds(im0 + t, n_re, stride=SCAN_PITCH), :] = h_im
        hs_ref[0] = h_re
        hs_ref[1] = h_im

    phase = lax.rem(s, 3)
    for r in range(3):
        pl.when(phase == r)(functools.partial(stages, bufs[r], bufs[(r + 2) % 3], bufs[(r + 1) % 3]))

    o_ref[...] = jnp.concatenate([y_ref[b] for b in range(nb)], axis=1)


def _s5_out_kernel(yf_ref, yb_ref, u_ref, d_ref, gw_ref, gb_ref, o_ref, gwbf_ref):
    @pl.when(pl.program_id(0) == 0)
    def _():
        gwbf_ref[...] = gw_ref[...].astype(BF16)

    y = yf_ref[...] + yb_ref[...] + u_ref[...] * d_ref[...]
    g = jax.nn.gelu(y)
    gate = jnp.dot(g.astype(BF16), gwbf_ref[...], preferred_element_type=F32) + gb_ref[...]
    o_ref[...] = (g * _sigmoid(gate)).astype(o_ref.dtype)


def _s5_mixer(z, col0, ch, scan_w, d_skip, glu_w, layer, glu_b, n_lat):
    m = z.shape[0]
    t_len = SEQ_TILE
    assert m - n_lat == t_len
    n_chunks = m // t_len
    ctx_chunk = n_chunks - 1
    cb = col0 // ch
    ar, ai, wb, wc = scan_w
    nb = ch // LANES
    n_re = ar.shape[1]
    assert t_len % wb.shape[1] == 0

    def scan(direction, order, reverse, name):
        which = 2 * layer + direction

        def const_spec(a):
            nd = a.ndim - 1
            return pl.BlockSpec((None,) + a.shape[1:], lambda s: (which,) + (0,) * nd)

        first = lambda s: order(jnp.minimum(s, n_chunks - 1))
        third = lambda s: order(jnp.clip(s - 2, 0, n_chunks - 1))
        return pl.pallas_call(
            functools.partial(_scan_kernel, reverse=reverse),
            grid=(n_chunks + 2,),
            in_specs=[pl.BlockSpec((t_len, ch), lambda s: (first(s), cb))]
            + [const_spec(a) for a in (wb, ar, ai, wc)],
            out_specs=pl.BlockSpec((t_len, ch), lambda s: (third(s), 0)),
            out_shape=jax.ShapeDtypeStruct((m, ch), F32),
            scratch_shapes=[pltpu.VMEM((2 * n_re * SCAN_PITCH, LANES), F32)] * 3 + [
                pltpu.VMEM((2, n_re, LANES), F32),
                pltpu.VMEM((nb, t_len, LANES), BF16),
                pltpu.VMEM((nb, t_len, LANES), F32),
            ],
            compiler_params=_cparams("arbitrary"),
            name=name,
        )(z, wb, ar, ai, wc)

    y_f = scan(0, lambda k: jnp.where(k == 0, ctx_chunk, k - 1), False, "s5_scan_fwd")
    y_b = scan(1, lambda k: jnp.where(k == 0, ctx_chunk, ctx_chunk - k), True, "s5_scan_bwd")

    tm = m // 8
    return pl.pallas_call(
        _s5_out_kernel,
        grid=(m // tm,),
        in_specs=[
            pl.BlockSpec((tm, ch), lambda i: (i, 0)),
            pl.BlockSpec((tm, ch), lambda i: (i, 0)),
            pl.BlockSpec((tm, ch), lambda i: (i, cb)),
            pl.BlockSpec((1, ch), lambda i: (0, 0)),
            pl.BlockSpec((None, ch, ch), lambda i: (layer, 0, 0)),
            pl.BlockSpec((1, ch), lambda i: (0, 0)),
        ],
        out_specs=pl.BlockSpec((tm, ch), lambda i: (i, 0)),
        out_shape=jax.ShapeDtypeStruct((m, ch), BF16),
        scratch_shapes=[pltpu.VMEM((ch, ch), BF16)],
        compiler_params=_cparams("arbitrary"),
        name="s5_out",
    )(y_f, y_b, z, d_skip.reshape(1, ch), glu_w, glu_b.reshape(1, ch))


def _layer(xa, cc, n_lat, layer, last, w_mod, b_mod, g_norm1, g_norm2, w_in, conv_w, conv_b, conv_ln_g,
           conv_ln_b, conv_pw_w, conv_pw_b, q_norm_g, k_norm_g, rpb, scan_w, ssm_d, glu_w, glu_b, w_out,
           w_ff1, w_ff3, w_ff2):
    unstacked = isinstance(xa, tuple)
    m_all = sum(a.shape[0] for a in xa) if unstacked else xa.shape[0]
    d = w_in.shape[1]
    conv_ch = conv_w.shape[1]
    ssm_ch = ssm_d.shape[0]
    na_width = (w_in.shape[2] - 2 * conv_ch - ssm_ch) // 3
    o1 = 2 * conv_ch
    o4 = o1 + 3 * na_width
    m_out = n_lat if last else m_all
    tm_all = m_all // 8
    tm_out = m_out // 8

    mod = _modulation(cc, w_mod, layer, b_mod)[:2]
    sh1, sc1, g1, sh2, sc2, g2 = [mod[:, i * d:(i + 1) * d] for i in range(6)]

    if unstacked:
        a, xa = _stack_norm_mod(xa[0], xa[1], g_norm1, sh1, sc1)
    else:
        a = _norm_mod(xa, g_norm1, sh1, sc1, n_lat, m_all)
    z = _proj(a, w_in, layer, F32, tm_all, 512)

    y_conv = _conv_module(z, conv_w, conv_b, conv_ln_g, conv_ln_b, conv_pw_w, layer, conv_pw_b, n_lat)
    y_na = _attention(z, q_norm_g, k_norm_g, _bias_table(rpb), n_lat, o1, na_width)
    y_s5 = _s5_mixer(z, o4, ssm_ch, scan_w, ssm_d, glu_w, layer, glu_b, n_lat)

    xa = _mix_out(y_conv, y_na, y_s5, w_out, layer, xa, g1, n_lat, m_out, tm_out, 512)
    hidden, w2b = _ffn_up(_norm_mod(xa, g_norm2, sh2, sc2, n_lat, m_out), w_ff1, w_ff3, w_ff2, layer, tm_out, 256)
    return _ffn_down(hidden, w2b, xa, g2, n_lat, tm_out // 2, 512)


def kernel(x, c, ctx, c_ctx, w_mod, b_mod, g_norm1, g_norm2, w_in, conv_w, conv_b, conv_ln_g, conv_ln_b,
           conv_pw_w, conv_pw_b, q_norm_g, k_norm_g, rpb, ssm_lambda_re, ssm_lambda_im, ssm_log_step,
           ssm_b_re, ssm_b_im, ssm_c_re, ssm_c_im, ssm_d, ssm_glu_w, ssm_glu_b, w_out, w_ff1, w_ff3, w_ff2):
    batch, seq, d = x.shape
    n_ctx = ctx.shape[1]
    assert batch == 1 and n_ctx == SEQ_TILE and seq % GRID_W == 0
    depth = w_mod.shape[0]
    xa = (x[0], ctx[0])
    cc = jnp.zeros((SUBLANES, d), F32).at[0].set(c[0]).at[1].set(c_ctx)
    flat = lambda a: a.reshape((-1,) + a.shape[2:])
    disc = _discretize(flat(ssm_lambda_re), flat(ssm_lambda_im), flat(ssm_log_step), flat(ssm_b_re),
                       flat(ssm_b_im))
    scan_w = jax.vmap(_scan_weights)(*disc, flat(ssm_c_re), flat(ssm_c_im))
    for l in range(depth):
        xa = _layer(xa, cc, seq, l, l == depth - 1, w_mod, b_mod[l], g_norm1[l], g_norm2[l], w_in, conv_w[l],
                    conv_b[l], conv_ln_g[l], conv_ln_b[l], conv_pw_w, conv_pw_b[l], q_norm_g[l], k_norm_g[l],
                    rpb[l], scan_w, ssm_d[l], ssm_glu_w, ssm_glu_b[l], w_out, w_ff1, w_ff3, w_ff2)
    return xa[None]
```

```python
import functools

import jax
import jax.numpy as jnp
from jax import lax
from jax.experimental import pallas as pl
from jax.experimental.pallas import tpu as pltpu

F32 = jnp.float32
BF16 = jnp.bfloat16

GRID_W = 64
WIN_H = 8
WIN_W = 16
HEAD_DIM = 128
CONV_K = 31
SSM_GROUP = 16
SSM_STATE = 64
EPS = 1e-6
NEG_INF = -1e30
LOG2E = 1.4426950408889634

LANES = 128
SUBLANES = 8
VMEM_LIMIT = 56 * 1024 * 1024

SEQ_TILE = 256
SCAN_PITCH = SEQ_TILE + SUBLANES
PIECE = 2 * LANES
HALO = 16


def _cparams(*sem):
    return pltpu.CompilerParams(dimension_semantics=sem, vmem_limit_bytes=VMEM_LIMIT)


def _sigmoid(x):
    return 1.0 / (1.0 + jnp.exp(-x))


def _silu(x):
    return x * _sigmoid(x)


def _mod_kernel(c_ref, w_ref, b_ref, o_ref):
    s = _silu(c_ref[...]).astype(BF16)
    o_ref[...] = jnp.dot(s, w_ref[...].astype(BF16), preferred_element_type=F32) + b_ref[...]


def _modulation(cc, w_mod, layer, b_mod):
    _, d, n = w_mod.shape
    tn = 512
    return pl.pallas_call(
        _mod_kernel,
        grid=(n // tn,),
        in_specs=[
            pl.BlockSpec((SUBLANES, d), lambda j: (0, 0)),
            pl.BlockSpec((None, d, tn), lambda j: (layer, 0, j)),
            pl.BlockSpec((1, tn), lambda j: (0, j)),
        ],
        out_specs=pl.BlockSpec((SUBLANES, tn), lambda j: (0, j)),
        out_shape=jax.ShapeDtypeStruct((SUBLANES, n), F32),
        compiler_params=_cparams("parallel"),
        name="adaln_mod",
    )(cc, w_mod, b_mod.reshape(1, n))


def _norm_mod_kernel(x_ref, g_ref, sh_ref, sc_ref, o_ref, *, n_lat_tiles):
    i = pl.program_id(0)
    x = x_ref[...]
    y = x * lax.rsqrt(jnp.mean(x * x, axis=-1, keepdims=True) + EPS) * g_ref[...]
    row = jnp.where(i >= n_lat_tiles, 1, 0)
    sh = sh_ref[pl.ds(row, 1), :]
    sc = sc_ref[pl.ds(row, 1), :]
    o_ref[...] = (y * (1.0 + sc) + sh).astype(BF16)


def _stack_norm_mod_kernel(lat_ref, ctx_ref, g_ref, sh_ref, sc_ref, o_ref, xa_ref, *, n_lat_tiles):
    i = pl.program_id(0)

    def emit(x, row):
        xa_ref[...] = x
        y = x * lax.rsqrt(jnp.mean(x * x, axis=-1, keepdims=True) + EPS) * g_ref[...]
        o_ref[...] = (y * (1.0 + sc_ref[row:row + 1, :]) + sh_ref[row:row + 1, :]).astype(BF16)

    @pl.when(i < n_lat_tiles)
    def _():
        emit(lat_ref[...], 0)

    @pl.when(i >= n_lat_tiles)
    def _():
        emit(ctx_ref[...], 1)


def _stack_norm_mod(lat, ctx, g, sh, sc):
    n_lat, d = lat.shape
    tm = SEQ_TILE
    n_lat_tiles = n_lat // tm
    m = n_lat + ctx.shape[0]
    return pl.pallas_call(
        functools.partial(_stack_norm_mod_kernel, n_lat_tiles=n_lat_tiles),
        grid=(m // tm,),
        in_specs=[
            pl.BlockSpec((tm, d), lambda i: (jnp.minimum(i, n_lat_tiles - 1), 0)),
            pl.BlockSpec((tm, d), lambda i: (jnp.maximum(i - n_lat_tiles, 0), 0)),
            pl.BlockSpec((1, d), lambda i: (0, 0)),
            pl.BlockSpec((2, d), lambda i: (0, 0)),
            pl.BlockSpec((2, d), lambda i: (0, 0)),
        ],
        out_specs=[pl.BlockSpec((tm, d), lambda i: (i, 0)), pl.BlockSpec((tm, d), lambda i: (i, 0))],
        out_shape=[jax.ShapeDtypeStruct((m, d), BF16), jax.ShapeDtypeStruct((m, d), F32)],
        compiler_params=_cparams("parallel"),
        name="stack_norm_mod",
    )(lat, ctx, g.reshape(1, d), sh, sc)


def _norm_mod(x, g, sh, sc, n_lat, m):
    d = x.shape[1]
    tm = SEQ_TILE
    return pl.pallas_call(
        functools.partial(_norm_mod_kernel, n_lat_tiles=n_lat // tm),
        grid=(m // tm,),
        in_specs=[
            pl.BlockSpec((tm, d), lambda i: (i, 0)),
            pl.BlockSpec((1, d), lambda i: (0, 0)),
            pl.BlockSpec((2, d), lambda i: (0, 0)),
            pl.BlockSpec((2, d), lambda i: (0, 0)),
        ],
        out_specs=pl.BlockSpec((tm, d), lambda i: (i, 0)),
        out_shape=jax.ShapeDtypeStruct((m, d), BF16),
        compiler_params=_cparams("parallel"),
        name="norm_mod",
    )(x, g.reshape(1, d), sh, sc)


def _proj_kernel(a_ref, w_ref, wo_ref, o_ref, wob_ref):
    wob_ref[...] = wo_ref[...].astype(BF16)
    o_ref[...] = jnp.dot(a_ref[...], w_ref[...].astype(BF16),
                         preferred_element_type=F32).astype(o_ref.dtype)


def _proj(a, w, w_out, layer, out_dtype, tm, tn):
    m, k = a.shape
    n = w.shape[2]
    k2, n2 = w_out.shape[1:]
    ni, nj = m // tm, n // tn
    rows = 2 * SUBLANES
    while k2 // rows > ni * nj:
        rows *= 2
    n_slices = k2 // rows
    assert n_slices * rows == k2
    blk = lambda i, j: jnp.minimum(i * nj + j, n_slices - 1)
    return pl.pallas_call(
        _proj_kernel,
        grid=(ni, nj),
        in_specs=[
            pl.BlockSpec((tm, k), lambda i, j: (i, 0)),
            pl.BlockSpec((None, k, tn), lambda i, j: (layer, 0, j)),
            pl.BlockSpec((None, rows, n2), lambda i, j: (layer, blk(i, j), 0)),
        ],
        out_specs=[
            pl.BlockSpec((tm, tn), lambda i, j: (i, j)),
            pl.BlockSpec((rows, n2), lambda i, j: (blk(i, j), 0)),
        ],
        out_shape=[jax.ShapeDtypeStruct((m, n), out_dtype), jax.ShapeDtypeStruct((k2, n2), BF16)],
        compiler_params=_cparams("arbitrary", "arbitrary"),
        name="proj",
    )(a, w, w_out)


def _gate_rows(gate_ref, tm, n_lat):
    rows = pl.program_id(0) * tm + lax.broadcasted_iota(jnp.int32, (tm, 1), 0)
    return jnp.where(rows < n_lat, gate_ref[0:1, :], gate_ref[1:2, :])


def _mix_out_kernel(a0_ref, a1_ref, a2_ref, w_ref, res_ref, gate_ref, o_ref, *, tm, n_lat):
    a = jnp.concatenate([a0_ref[...], a1_ref[...], a2_ref[...]], axis=1)
    acc = jnp.dot(a, w_ref[...], preferred_element_type=F32)
    o_ref[...] = res_ref[...] + _gate_rows(gate_ref, tm, n_lat) * acc


def _mix_out(a0, a1, a2, w, res, gate, n_lat, m, tm, tn):
    k, n = w.shape
    return pl.pallas_call(
        functools.partial(_mix_out_kernel, tm=tm, n_lat=n_lat),
        grid=(m // tm, n // tn),
        in_specs=[
            pl.BlockSpec((tm, a0.shape[1]), lambda i, j: (i, 0)),
            pl.BlockSpec((tm, a1.shape[1]), lambda i, j: (i, 0)),
            pl.BlockSpec((tm, a2.shape[1]), lambda i, j: (i, 0)),
            pl.BlockSpec((k, tn), lambda i, j: (0, j)),
            pl.BlockSpec((tm, tn), lambda i, j: (i, j)),
            pl.BlockSpec((2, tn), lambda i, j: (0, j)),
        ],
        out_specs=pl.BlockSpec((tm, tn), lambda i, j: (i, j)),
        out_shape=jax.ShapeDtypeStruct((m, n), F32),
        compiler_params=_cparams("parallel", "arbitrary"),
        name="mix_out",
    )(a0, a1, a2, w, res, gate)


def _ffn_up_kernel(a_ref, w1_ref, w3_ref, w2_ref, o_ref, w2b_ref):
    w2b_ref[...] = w2_ref[...].astype(BF16)
    t = jnp.dot(a_ref[...], w1_ref[...].astype(BF16), preferred_element_type=F32)
    u = jnp.dot(a_ref[...], w3_ref[...].astype(BF16), preferred_element_type=F32)
    o_ref[...] = (_silu(t) * u).astype(o_ref.dtype)


def _ffn_up(a, w1, w3, w2, layer, tm, tn):
    m, k = a.shape
    n = w1.shape[2]
    n2 = w2.shape[2]
    ni, nj = m // tm, n // tn
    rows = n // (ni * nj)
    assert rows * ni * nj == n and rows % (2 * SUBLANES) == 0
    return pl.pallas_call(
        _ffn_up_kernel,
        grid=(ni, nj),
        in_specs=[
            pl.BlockSpec((tm, k), lambda i, j: (i, 0)),
            pl.BlockSpec((None, k, tn), lambda i, j: (layer, 0, j)),
            pl.BlockSpec((None, k, tn), lambda i, j: (layer, 0, j)),
            pl.BlockSpec((None, rows, n2), lambda i, j: (layer, i * nj + j, 0)),
        ],
        out_specs=[
            pl.BlockSpec((tm, tn), lambda i, j: (i, j)),
            pl.BlockSpec((rows, n2), lambda i, j: (i * nj + j, 0)),
        ],
        out_shape=[jax.ShapeDtypeStruct((m, n), BF16), jax.ShapeDtypeStruct((n, n2), BF16)],
        compiler_params=_cparams("parallel", "arbitrary"),
        name="ffn_up",
    )(a, w1, w3, w2)


def _ffn_down_kernel(a_ref, w_ref, res_ref, gate_ref, o_ref, *, tm, n_lat):
    acc = jnp.dot(a_ref[...], w_ref[...], preferred_element_type=F32)
    o_ref[...] = res_ref[...] + _gate_rows(gate_ref, tm, n_lat) * acc


def _ffn_down(a, w, res, gate, n_lat, tm, tn):
    m, k = a.shape
    n = w.shape[1]
    return pl.pallas_call(
        functools.partial(_ffn_down_kernel, tm=tm, n_lat=n_lat),
        grid=(m // tm, n // tn),
        in_specs=[
            pl.BlockSpec((tm, k), lambda i, j: (i, 0)),
            pl.BlockSpec((k, tn), lambda i, j: (0, j)),
            pl.BlockSpec((tm, tn), lambda i, j: (i, j)),
            pl.BlockSpec((2, tn), lambda i, j: (0, j)),
        ],
        out_specs=pl.BlockSpec((tm, tn), lambda i, j: (i, j)),
        out_shape=jax.ShapeDtypeStruct((m, n), F32),
        compiler_params=_cparams("parallel", "arbitrary"),
        name="ffn_down",
    )(a, w, res, gate)


def _conv_kernel(main_ref, prev_ref, next_ref, cw_ref, cb_ref, lng_ref, lnb_ref, pw_ref, pwb_ref,
                 o_ref, uext_ref, yc_ref, wbf_ref, *, n_lat_tiles, n_tiles):
    i = pl.program_id(0)
    tl = main_ref.shape[0]
    ch = o_ref.shape[1]
    nblk = ch // LANES

    @pl.when(i == 0)
    def _():
        wbf_ref[...] = pw_ref[...].astype(BF16)

    def glu(a):
        return a[:, :ch] * _sigmoid(a[:, ch:])

    is_start = jnp.logical_or(i == 0, i == n_lat_tiles)
    is_end = jnp.logical_or(i == n_lat_tiles - 1, i == n_tiles - 1)
    u = glu(main_ref[...])
    up = jnp.where(is_start, 0.0, glu(prev_ref[...]))
    un = jnp.where(is_end, 0.0, glu(next_ref[...]))
    for c in range(nblk):
        sl = slice(c * LANES, (c + 1) * LANES)
        uext_ref[c, 0:HALO, :] = up[:, sl]
        uext_ref[c, HALO:HALO + tl, :] = u[:, sl]
        uext_ref[c, HALO + tl:2 * HALO + tl, :] = un[:, sl]

    def chan_block(c, carry):
        acc = jnp.zeros((tl, LANES), F32)
        for k in range(CONV_K):
            acc = acc + uext_ref[c, pl.ds(k + HALO - CONV_K // 2, tl), :] * cw_ref[c, k:k + 1, :]
        yc_ref[c] = acc + cb_ref[c]
        return carry

    lax.fori_loop(0, nblk, chan_block, 0)

    y = jnp.concatenate([yc_ref[c] for c in range(nblk)], axis=1)
    mu = jnp.mean(y, axis=-1, keepdims=True)
    yd = y - mu
    var = jnp.mean(yd * yd, axis=-1, keepdims=True)
    yn = yd * lax.rsqrt(var + EPS) * lng_ref[...] + lnb_ref[...]
    s = _silu(yn).astype(BF16)
    o_ref[...] = (jnp.dot(s, wbf_ref[...], preferred_element_type=F32) + pwb_ref[...]).astype(o_ref.dtype)


def _conv_module(z, conv_w, conv_b, ln_g, ln_b, pw_w, layer, pw_b, n_lat):
    m = z.shape[0]
    ch = conv_w.shape[1]
    tl = SEQ_TILE
    nblk = ch // LANES
    n_tiles = m // tl
    hb = tl // HALO
    cw = conv_w.reshape(CONV_K, nblk, LANES).transpose(1, 0, 2)
    cb = conv_b.reshape(nblk, 1, LANES)
    return pl.pallas_call(
        functools.partial(_conv_kernel, n_lat_tiles=n_lat // tl, n_tiles=n_tiles),
        grid=(n_tiles,),
        in_specs=[
            pl.BlockSpec((tl, 2 * ch), lambda i: (i, 0)),
            pl.BlockSpec((HALO, 2 * ch), lambda i: (jnp.maximum(i * hb - 1, 0), 0)),
            pl.BlockSpec((HALO, 2 * ch), lambda i: (jnp.minimum((i + 1) * hb, n_tiles * hb - 1), 0)),
            pl.BlockSpec((nblk, CONV_K, LANES), lambda i: (0, 0, 0)),
            pl.BlockSpec((nblk, 1, LANES), lambda i: (0, 0, 0)),
            pl.BlockSpec((1, ch), lambda i: (0, 0)),
            pl.BlockSpec((1, ch), lambda i: (0, 0)),
            pl.BlockSpec((None, ch, ch), lambda i: (layer, 0, 0)),
            pl.BlockSpec((1, ch), lambda i: (0, 0)),
        ],
        out_specs=pl.BlockSpec((tl, ch), lambda i: (i, 0)),
        out_shape=jax.ShapeDtypeStruct((m, ch), BF16),
        scratch_shapes=[
            pltpu.VMEM((nblk, tl + 2 * HALO, LANES), F32),
            pltpu.VMEM((nblk, tl, LANES), F32),
            pltpu.VMEM((ch, ch), BF16),
        ],
        compiler_params=_cparams("arbitrary"),
        name="conv_module",
    )(z, z, z, cw, cb, ln_g.reshape(1, ch), ln_b.reshape(1, ch), pw_w, pw_b.reshape(1, ch))


Q_ROWS = 4
BAND_ROWS = Q_ROWS + WIN_H
_BLOCK_CASES = (
    (0, lambda a: max(a - WIN_H // 2, 0)),
    (-WIN_H // 2, lambda a: a),
    (-WIN_H, lambda a: min(a + WIN_H // 2, Q_ROWS)),
)


def _bias_table_kernel(rpb_ref, o_ref):
    h = pl.program_id(0)
    qc = lax.broadcasted_iota(jnp.int32, (GRID_W, GRID_W), 0)
    kc = lax.broadcasted_iota(jnp.int32, (GRID_W, GRID_W), 1)
    qstart = jnp.clip(qc - WIN_W // 2, 0, GRID_W - WIN_W)
    valid = jnp.logical_and(kc >= qstart, kc < qstart + WIN_W)
    dc = jnp.where(valid, kc - qc + WIN_W - 1, -1)
    tabs = []
    for dr in range(2 * WIN_H - 1):
        t = jnp.full((GRID_W, GRID_W), NEG_INF, F32)
        for d in range(2 * WIN_W - 1):
            t = jnp.where(dc == d, rpb_ref[h, dr, d] * LOG2E, t)
        tabs.append(t)
    masked = jnp.full((GRID_W, GRID_W), NEG_INF, F32)
    for case, (band0, first_valid) in enumerate(_BLOCK_CASES):
        for a in range(Q_ROWS):
            blocks = []
            for i in range(BAND_ROWS):
                in_window = first_valid(a) <= i < first_valid(a) + WIN_H
                dr = band0 + i - a + WIN_H - 1
                blocks.append(tabs[dr] if in_window else masked)
            o_ref[0, case, a * GRID_W:(a + 1) * GRID_W, :] = jnp.concatenate(blocks, axis=1)


def _bias_table(rpb):
    heads = rpb.shape[0]
    shape = (len(_BLOCK_CASES), Q_ROWS * GRID_W, BAND_ROWS * GRID_W)
    return pl.pallas_call(
        _bias_table_kernel,
        grid=(heads,),
        in_specs=[pl.BlockSpec(memory_space=pltpu.SMEM)],
        out_specs=pl.BlockSpec((1,) + shape, lambda h: (h, 0, 0, 0)),
        out_shape=jax.ShapeDtypeStruct((heads,) + shape, F32),
        compiler_params=_cparams("parallel"),
        name="bias_table",
    )(rpb)


def _attn_kernel(q_ref, k_ref, v_ref, qg_ref, kg_ref, bias_ref, o_ref, kn_ref, vb_ref, *, rows):
    m = q_ref.shape[0]
    scale = HEAD_DIM ** -0.5
    band = BAND_ROWS * GRID_W
    nt = (((1,), (1,)), ((), ()))

    def rms(x, g):
        return x * lax.rsqrt(jnp.mean(x * x, axis=-1, keepdims=True) + EPS) * g

    def prep(c, carry):
        r = pl.multiple_of(c * SEQ_TILE, SEQ_TILE)
        kn_ref[pl.ds(r, SEQ_TILE), :] = rms(k_ref[pl.ds(r, SEQ_TILE), :], kg_ref[...]).astype(BF16)
        vb_ref[pl.ds(r, SEQ_TILE), :] = v_ref[pl.ds(r, SEQ_TILE), :].astype(BF16)
        return carry

    lax.fori_loop(0, m // SEQ_TILE, prep, 0, unroll=3)

    n_lat = rows * GRID_W
    kc = kn_ref[n_lat:m, :]
    vc = vb_ref[n_lat:m, :]

    qc = rms(q_ref[n_lat:m, :], qg_ref[...]).astype(BF16)
    s = lax.dot_general(qc, kc, nt, preferred_element_type=F32) * scale
    p = jnp.exp(s - jnp.max(s, axis=-1, keepdims=True))
    l = jnp.sum(p, axis=-1, keepdims=True)
    o_ref[n_lat:m, :] = (jnp.dot(p.astype(BF16), vc, preferred_element_type=F32) / l).astype(o_ref.dtype)

    n_blocks = rows // Q_ROWS
    nq = Q_ROWS * GRID_W

    def block(bq, carry):
        r = bq * Q_ROWS
        band0 = jnp.clip(r - WIN_H // 2, 0, rows - BAND_ROWS)
        case = jnp.where(bq == 0, 0, jnp.where(bq == n_blocks - 1, 2, 1))
        qs = pl.multiple_of(r * GRID_W, nq)
        ks = pl.multiple_of(band0 * GRID_W, (WIN_H // 2) * GRID_W)
        q = rms(q_ref[pl.ds(qs, nq), :], qg_ref[...]).astype(BF16)
        kb = kn_ref[pl.ds(ks, band), :]
        s_loc = lax.dot_general(q, kb, nt, preferred_element_type=F32) * (scale * LOG2E) + bias_ref[0, case]
        s_ctx = lax.dot_general(q, kc, nt, preferred_element_type=F32) * (scale * LOG2E)
        mx = jnp.maximum(jnp.max(s_loc, axis=-1, keepdims=True), jnp.max(s_ctx, axis=-1, keepdims=True))
        p_loc = jnp.exp2(s_loc - mx)
        p_ctx = jnp.exp2(s_ctx - mx)
        den = jnp.sum(p_loc, axis=-1, keepdims=True) + jnp.sum(p_ctx, axis=-1, keepdims=True)
        o = jnp.dot(p_loc.astype(BF16), vb_ref[pl.ds(ks, band), :], preferred_element_type=F32)
        o = o + jnp.dot(p_ctx.astype(BF16), vc, preferred_element_type=F32)
        o_ref[pl.ds(qs, nq), :] = (o / den).astype(o_ref.dtype)
        return carry

    lax.fori_loop(0, n_blocks, block, 0, unroll=4)


def _attention(z, q_g, k_g, bias_tab, n_lat, col0, width):
    m = z.shape[0]
    heads = width // HEAD_DIM
    rows = n_lat // GRID_W
    assert rows % Q_ROWS == 0 and rows >= 2 * BAND_ROWS and Q_ROWS % (WIN_H // 2) == 0
    cb = col0 // HEAD_DIM
    return pl.pallas_call(
        functools.partial(_attn_kernel, rows=rows),
        grid=(heads,),
        in_specs=[
            pl.BlockSpec((m, HEAD_DIM), lambda h: (0, cb + h)),
            pl.BlockSpec((m, HEAD_DIM), lambda h: (0, cb + heads + h)),
            pl.BlockSpec((m, HEAD_DIM), lambda h: (0, cb + 2 * heads + h)),
            pl.BlockSpec((1, HEAD_DIM), lambda h: (0, 0)),
            pl.BlockSpec((1, HEAD_DIM), lambda h: (0, 0)),
            pl.BlockSpec((1,) + bias_tab.shape[1:], lambda h: (h, 0, 0, 0)),
        ],
        out_specs=pl.BlockSpec((m, HEAD_DIM), lambda h: (0, h)),
        out_shape=jax.ShapeDtypeStruct((m, width), BF16),
        scratch_shapes=[pltpu.VMEM((m, HEAD_DIM), BF16), pltpu.VMEM((m, HEAD_DIM), BF16)],
        compiler_params=_cparams("parallel"),
        name="nbr_attention",
    )(z, z, z, q_g.reshape(1, HEAD_DIM), k_g.reshape(1, HEAD_DIM), bias_tab)


def _discretize_kernel(lr_ref, li_ref, ls_ref, br_ref, bi_ref, ar_ref, ai_ref, bbr_ref, bbi_ref):
    lr = lr_ref[0]
    li = li_ref[0]
    step = jnp.exp(ls_ref[0])
    mag = jnp.exp(lr * step)
    ang = li * step
    a_re = mag * jnp.cos(ang)
    a_im = mag * jnp.sin(ang)
    inv = 1.0 / (lr * lr + li * li)
    f_re = ((a_re - 1.0) * lr + a_im * li) * inv
    f_im = (a_im * lr - (a_re - 1.0) * li) * inv
    br = br_ref[0]
    bi = bi_ref[0]
    ar_ref[0] = a_re
    ai_ref[0] = a_im
    bbr_ref[0] = f_re * br - f_im * bi
    bbi_ref[0] = f_re * bi + f_im * br


def _discretize(lam_re, lam_im, log_step, b_re, b_im):
    n_dir, g, n = lam_re.shape
    p = b_re.shape[-1]
    rep = lambda a: jnp.repeat(a, p, axis=1)
    lr = rep(lam_re)
    li = rep(lam_im)
    ls = rep(jnp.broadcast_to(log_step[:, :, None], (n_dir, g, n)))
    br = b_re.transpose(0, 1, 3, 2).reshape(n_dir, g * p, n)
    bi = b_im.transpose(0, 1, 3, 2).reshape(n_dir, g * p, n)
    spec = pl.BlockSpec((1, g * p, n), lambda d: (d, 0, 0))
    shp = jax.ShapeDtypeStruct((n_dir, g * p, n), F32)
    return pl.pallas_call(
        _discretize_kernel,
        grid=(n_dir,),
        in_specs=[spec] * 5,
        out_specs=[spec] * 4,
        out_shape=[shp] * 4,
        compiler_params=_cparams("parallel"),
        name="s5_discretize",
    )(lr, li, ls, br, bi)


def _scan_weights(a_re, a_im, bbt_re, bbt_im, c_re, c_im):
    g, n = a_re.shape[0] // SSM_GROUP, a_re.shape[1]
    gb = LANES // SSM_GROUP
    nb = g // gb
    eye = jnp.eye(gb, dtype=F32)
    ar = a_re[::SSM_GROUP].reshape(g * n // LANES, LANES)
    ai = a_im[::SSM_GROUP].reshape(g * n // LANES, LANES)

    def in_mat(bbt):
        blk = bbt.reshape(nb, gb, SSM_GROUP, n)
        return jnp.einsum("bgpn,gh->bgphn", blk, eye).reshape(nb, LANES, gb * n)

    def out_mat(c):
        blk = c.reshape(nb, gb, SSM_GROUP, n)
        return jnp.einsum("bgpn,gh->bgnhp", blk, eye).reshape(nb, gb * n, LANES)

    wb = jnp.concatenate([in_mat(bbt_re), in_mat(bbt_im)], axis=2)
    wc = jnp.concatenate([out_mat(c_re), -out_mat(c_im)], axis=1)
    pieces = wb.shape[2] // PIECE
    wb = wb.reshape(nb, LANES, pieces, PIECE).transpose(0, 2, 1, 3).reshape(nb * pieces, LANES, PIECE)
    wc = wc.reshape(nb * pieces, PIECE, LANES)
    return ar, ai, wb.astype(BF16), wc.astype(BF16)


def _scan_kernel(u_ref, wb_ref, ar_ref, ai_ref, wc_ref, o_ref, xs0_ref, xs1_ref, xs2_ref, hs_ref, ub_ref,
                 y_ref, *, reverse):
    s = pl.program_id(0)
    t_len = u_ref.shape[0]
    n_piece = wb_ref.shape[0]
    nb = ub_ref.shape[0]
    ppb = n_piece // nb
    n_re = ar_ref.shape[0]
    spb = n_re // nb
    steps = t_len // n_piece
    im0 = n_re * SCAN_PITCH
    bufs = (xs0_ref, xs1_ref, xs2_ref)

    @pl.when(s == 0)
    def _():
        for ref in bufs:
            ref[...] = jnp.zeros_like(ref)
        hs_ref[...] = jnp.zeros_like(hs_ref)

    u = u_ref[...]
    for b in range(nb):
        ub_ref[b] = u[:, b * LANES:(b + 1) * LANES].astype(BF16)
    y_ref[...] = jnp.zeros_like(y_ref)

    def stages(nxt_ref, cur_ref, prv_ref):
        ar = ar_ref[...]
        ai = ai_ref[...]
        h_re = hs_ref[0]
        h_im = hs_ref[1]
        for j in range(n_piece):
            b = j // ppb
            q = j % ppb
            row0 = ((q // 2) * n_re + b * spb + (q % 2) * 2) * SCAN_PITCH
            hp = jnp.concatenate(
                [prv_ref[pl.ds(row0, t_len), :], prv_ref[pl.ds(row0 + SCAN_PITCH, t_len), :]], axis=1)
            y_ref[b] += jnp.dot(hp.astype(BF16), wc_ref[j], preferred_element_type=F32)
            xb = jnp.dot(ub_ref[b], wb_ref[j], preferred_element_type=F32)
            nxt_ref[pl.ds(row0, t_len), :] = xb[:, :LANES]
            nxt_ref[pl.ds(row0 + SCAN_PITCH, t_len), :] = xb[:, LANES:]
            for jj in range(j * steps, (j + 1) * steps):
                t = (t_len - 1 - jj) if reverse else jj
                x_re = cur_ref[pl.ds(t, n_re, stride=SCAN_PITCH), :]
                x_im = cur_ref[pl.ds(im0 + t, n_re, stride=SCAN_PITCH), :]
                h_re, h_im = ar * h_re - ai * h_im + x_re, ar * h_im + ai * h_re + x_im
                cur_ref[pl.ds(t, n_re, stride=SCAN_PITCH), :] = h_re
                cur_ref[pl.ds(im0 + t, n_re, stride=SCAN_PITCH), :] = h_im
        hs_ref[0] = h_re
        hs_ref[1] = h_im

    phase = lax.rem(s, 3)
    for r in range(3):
        pl.when(phase == r)(functools.partial(stages, bufs[r], bufs[(r + 2) % 3], bufs[(r + 1) % 3]))

    o_ref[...] = jnp.concatenate([y_ref[b] for b in range(nb)], axis=1)


def _s5_out_kernel(yf_ref, yb_ref, u_ref, d_ref, gw_ref, gb_ref, o_ref, gwbf_ref):
    @pl.when(pl.program_id(0) == 0)
    def _():
        gwbf_ref[...] = gw_ref[...].astype(BF16)

    y = yf_ref[...] + yb_ref[...] + u_ref[...] * d_ref[...]
    g = jax.nn.gelu(y)
    gate = jnp.dot(g.astype(BF16), gwbf_ref[...], preferred_element_type=F32) + gb_ref[...]
    o_ref[...] = (g * _sigmoid(gate)).astype(o_ref.dtype)


def _s5_mixer(z, col0, ch, scan_w, d_skip, glu_w, layer, glu_b, n_lat):
    m = z.shape[0]
    t_len = SEQ_TILE
    assert m - n_lat == t_len
    n_chunks = m // t_len
    ctx_chunk = n_chunks - 1
    cb = col0 // ch
    ar, ai, wb, wc = scan_w
    nb = ch // LANES
    n_re = ar.shape[1]
    assert t_len % wb.shape[1] == 0

    def scan(direction, order, reverse, name):
        which = 2 * layer + direction

        def const_spec(a):
            nd = a.ndim - 1
            return pl.BlockSpec((None,) + a.shape[1:], lambda s: (which,) + (0,) * nd)

        first = lambda s: order(jnp.minimum(s, n_chunks - 1))
        third = lambda s: order(jnp.clip(s - 2, 0, n_chunks - 1))
        return pl.pallas_call(
            functools.partial(_scan_kernel, reverse=reverse),
            grid=(n_chunks + 2,),
            in_specs=[pl.BlockSpec((t_len, ch), lambda s: (first(s), cb))]
            + [const_spec(a) for a in (wb, ar, ai, wc)],
            out_specs=pl.BlockSpec((t_len, ch), lambda s: (third(s), 0)),
            out_shape=jax.ShapeDtypeStruct((m, ch), F32),
            scratch_shapes=[pltpu.VMEM((2 * n_re * SCAN_PITCH, LANES), F32)] * 3 + [
                pltpu.VMEM((2, n_re, LANES), F32),
                pltpu.VMEM((nb, t_len, LANES), BF16),
                pltpu.VMEM((nb, t_len, LANES), F32),
            ],
            compiler_params=_cparams("arbitrary"),
            name=name,
        )(z, wb, ar, ai, wc)

    y_f = scan(0, lambda k: jnp.where(k == 0, ctx_chunk, k - 1), False, "s5_scan_fwd")
    y_b = scan(1, lambda k: jnp.where(k == 0, ctx_chunk, ctx_chunk - k), True, "s5_scan_bwd")

    tm = m // 8
    return pl.pallas_call(
        _s5_out_kernel,
        grid=(m // tm,),
        in_specs=[
            pl.BlockSpec((tm, ch), lambda i: (i, 0)),
            pl.BlockSpec((tm, ch), lambda i: (i, 0)),
            pl.BlockSpec((tm, ch), lambda i: (i, cb)),
            pl.BlockSpec((1, ch), lambda i: (0, 0)),
            pl.BlockSpec((None, ch, ch), lambda i: (layer, 0, 0)),
            pl.BlockSpec((1, ch), lambda i: (0, 0)),
        ],
        out_specs=pl.BlockSpec((tm, ch), lambda i: (i, 0)),
        out_shape=jax.ShapeDtypeStruct((m, ch), BF16),
        scratch_shapes=[pltpu.VMEM((ch, ch), BF16)],
        compiler_params=_cparams("arbitrary"),
        name="s5_out",
    )(y_f, y_b, z, d_skip.reshape(1, ch), glu_w, glu_b.reshape(1, ch))


def _layer(xa, cc, n_lat, layer, last, w_mod, b_mod, g_norm1, g_norm2, w_in, conv_w, conv_b, conv_ln_g,
           conv_ln_b, conv_pw_w, conv_pw_b, q_norm_g, k_norm_g, rpb, scan_w, ssm_d, glu_w, glu_b, w_out,
           w_ff1, w_ff3, w_ff2):
    unstacked = isinstance(xa, tuple)
    m_all = sum(a.shape[0] for a in xa) if unstacked else xa.shape[0]
    d = w_in.shape[1]
    conv_ch = conv_w.shape[1]
    ssm_ch = ssm_d.shape[0]
    na_width = (w_in.shape[2] - 2 * conv_ch - ssm_ch) // 3
    o1 = 2 * conv_ch
    o4 = o1 + 3 * na_width
    m_out = n_lat if last else m_all
    tm_all = m_all // 8
    tm_out = m_out // 8

    mod = _modulation(cc, w_mod, layer, b_mod)[:2]
    sh1, sc1, g1, sh2, sc2, g2 = [mod[:, i * d:(i + 1) * d] for i in range(6)]

    if unstacked:
        a, xa = _stack_norm_mod(xa[0], xa[1], g_norm1, sh1, sc1)
    else:
        a = _norm_mod(xa, g_norm1, sh1, sc1, n_lat, m_all)
    z, w_out_b = _proj(a, w_in, w_out, layer, F32, tm_all, 512)

    y_conv = _conv_module(z, conv_w, conv_b, conv_ln_g, conv_ln_b, conv_pw_w, layer, conv_pw_b, n_lat)
    y_na = _attention(z, q_norm_g, k_norm_g, _bias_table(rpb), n_lat, o1, na_width)
    y_s5 = _s5_mixer(z, o4, ssm_ch, scan_w, ssm_d, glu_w, layer, glu_b, n_lat)

    xa = _mix_out(y_conv, y_na, y_s5, w_out_b, xa, g1, n_lat, m_out, tm_out, 512)
    a = _norm_mod(xa, g_norm2, sh2, sc2, n_lat, m_out)
    hidden, w2b = _ffn_up(a, w_ff1, w_ff3, w_ff2, layer, tm_out, 256)
    return _ffn_down(hidden, w2b, xa, g2, n_lat, tm_out // 2, 512)


def kernel(x, c, ctx, c_ctx, w_mod, b_mod, g_norm1, g_norm2, w_in, conv_w, conv_b, conv_ln_g, conv_ln_b,
           conv_pw_w, conv_pw_b, q_norm_g, k_norm_g, rpb, ssm_lambda_re, ssm_lambda_im, ssm_log_step,
           ssm_b_re, ssm_b_im, ssm_c_re, ssm_c_im, ssm_d, ssm_glu_w, ssm_glu_b, w_out, w_ff1, w_ff3, w_ff2):
    batch, seq, d = x.shape
    n_ctx = ctx.shape[1]
    assert batch == 1 and n_ctx == SEQ_TILE and seq % GRID_W == 0
    depth = w_mod.shape[0]
    xa = (x[0], ctx[0])
    cc = jnp.zeros((SUBLANES, d), F32).at[0].set(c[0]).at[1].set(c_ctx)
    flat = lambda a: a.reshape((-1,) + a.shape[2:])
    disc = _discretize(flat(ssm_lambda_re), flat(ssm_lambda_im), flat(ssm_log_step), flat(ssm_b_re),
                       flat(ssm_b_im))
    scan_w = jax.vmap(_scan_weights)(*disc, flat(ssm_c_re), flat(ssm_c_im))
    for l in range(depth):
        xa = _layer(xa, cc, seq, l, l == depth - 1, w_mod, b_mod[l], g_norm1[l], g_norm2[l], w_in, conv_w[l],
                    conv_b[l], conv_ln_g[l], conv_ln_b[l], conv_pw_w, conv_pw_b[l], q_norm_g[l], k_norm_g[l],
                    rpb[l], scan_w, ssm_d[l], ssm_glu_w, ssm_glu_b[l], w_out, w_ff1, w_ff3, w_ff2)
    return xa[None]
```

```python
import functools

import jax
import jax.numpy as jnp
from jax import lax
from jax.experimental import pallas as pl
from jax.experimental.pallas import tpu as pltpu

F32 = jnp.float32
BF16 = jnp.bfloat16

GRID_W = 64
WIN_H = 8
WIN_W = 16
HEAD_DIM = 128
CONV_K = 31
SSM_GROUP = 16
SSM_STATE = 64
EPS = 1e-6
NEG_INF = -1e30
LOG2E = 1.4426950408889634

LANES = 128
SUBLANES = 8
VMEM_LIMIT = 56 * 1024 * 1024

SEQ_TILE = 256
SCAN_PITCH = SEQ_TILE + SUBLANES
PIECE = 2 * LANES
HALO = 16


def _cparams(*sem):
    return pltpu.CompilerParams(dimension_semantics=sem, vmem_limit_bytes=VMEM_LIMIT)


def _sigmoid(x):
    return 1.0 / (1.0 + jnp.exp(-x))


def _silu(x):
    return x * _sigmoid(x)


def _mod_kernel(c_ref, w_ref, b_ref, o_ref):
    s = _silu(c_ref[...]).astype(BF16)
    o_ref[...] = jnp.dot(s, w_ref[...].astype(BF16), preferred_element_type=F32) + b_ref[...]


def _modulation(cc, w_mod, layer, b_mod):
    _, d, n = w_mod.shape
    tn = 512
    return pl.pallas_call(
        _mod_kernel,
        grid=(n // tn,),
        in_specs=[
            pl.BlockSpec((SUBLANES, d), lambda j: (0, 0)),
            pl.BlockSpec((None, d, tn), lambda j: (layer, 0, j)),
            pl.BlockSpec((1, tn), lambda j: (0, j)),
        ],
        out_specs=pl.BlockSpec((SUBLANES, tn), lambda j: (0, j)),
        out_shape=jax.ShapeDtypeStruct((SUBLANES, n), F32),
        compiler_params=_cparams("parallel"),
        name="adaln_mod",
    )(cc, w_mod, b_mod.reshape(1, n))


def _norm_mod_kernel(x_ref, g_ref, sh_ref, sc_ref, o_ref, *, n_lat_tiles):
    i = pl.program_id(0)
    x = x_ref[...]
    y = x * lax.rsqrt(jnp.mean(x * x, axis=-1, keepdims=True) + EPS) * g_ref[...]
    row = jnp.where(i >= n_lat_tiles, 1, 0)
    sh = sh_ref[pl.ds(row, 1), :]
    sc = sc_ref[pl.ds(row, 1), :]
    o_ref[...] = (y * (1.0 + sc) + sh).astype(BF16)


def _stack_norm_mod_kernel(lat_ref, ctx_ref, g_ref, sh_ref, sc_ref, o_ref, xa_ref, *, n_lat_tiles):
    i = pl.program_id(0)

    def emit(x, row):
        xa_ref[...] = x
        y = x * lax.rsqrt(jnp.mean(x * x, axis=-1, keepdims=True) + EPS) * g_ref[...]
        o_ref[...] = (y * (1.0 + sc_ref[row:row + 1, :]) + sh_ref[row:row + 1, :]).astype(BF16)

    @pl.when(i < n_lat_tiles)
    def _():
        emit(lat_ref[...], 0)

    @pl.when(i >= n_lat_tiles)
    def _():
        emit(ctx_ref[...], 1)


def _stack_norm_mod(lat, ctx, g, sh, sc):
    n_lat, d = lat.shape
    tm = SEQ_TILE
    n_lat_tiles = n_lat // tm
    m = n_lat + ctx.shape[0]
    return pl.pallas_call(
        functools.partial(_stack_norm_mod_kernel, n_lat_tiles=n_lat_tiles),
        grid=(m // tm,),
        in_specs=[
            pl.BlockSpec((tm, d), lambda i: (jnp.minimum(i, n_lat_tiles - 1), 0)),
            pl.BlockSpec((tm, d), lambda i: (jnp.maximum(i - n_lat_tiles, 0), 0)),
            pl.BlockSpec((1, d), lambda i: (0, 0)),
            pl.BlockSpec((2, d), lambda i: (0, 0)),
            pl.BlockSpec((2, d), lambda i: (0, 0)),
        ],
        out_specs=[pl.BlockSpec((tm, d), lambda i: (i, 0)), pl.BlockSpec((tm, d), lambda i: (i, 0))],
        out_shape=[jax.ShapeDtypeStruct((m, d), BF16), jax.ShapeDtypeStruct((m, d), F32)],
        compiler_params=_cparams("parallel"),
        name="stack_norm_mod",
    )(lat, ctx, g.reshape(1, d), sh, sc)


def _norm_mod(x, g, sh, sc, n_lat, m):
    d = x.shape[1]
    tm = SEQ_TILE
    return pl.pallas_call(
        functools.partial(_norm_mod_kernel, n_lat_tiles=n_lat // tm),
        grid=(m // tm,),
        in_specs=[
            pl.BlockSpec((tm, d), lambda i: (i, 0)),
            pl.BlockSpec((1, d), lambda i: (0, 0)),
            pl.BlockSpec((2, d), lambda i: (0, 0)),
            pl.BlockSpec((2, d), lambda i: (0, 0)),
        ],
        out_specs=pl.BlockSpec((tm, d), lambda i: (i, 0)),
        out_shape=jax.ShapeDtypeStruct((m, d), BF16),
        compiler_params=_cparams("parallel"),
        name="norm_mod",
    )(x, g.reshape(1, d), sh, sc)


def _proj_kernel(a_ref, w_ref, wo_ref, o_ref, wob_ref):
    wob_ref[...] = wo_ref[...].astype(BF16)
    o_ref[...] = jnp.dot(a_ref[...], w_ref[...].astype(BF16),
                         preferred_element_type=F32).astype(o_ref.dtype)


def _proj(a, w, w_out, layer, out_dtype, tm, tn):
    m, k = a.shape
    n = w.shape[2]
    k2, n2 = w_out.shape[1:]
    ni, nj = m // tm, n // tn
    rows = 2 * SUBLANES
    while k2 // rows > ni * nj:
        rows *= 2
    n_slices = k2 // rows
    assert n_slices * rows == k2
    blk = lambda i, j: jnp.minimum(i * nj + j, n_slices - 1)
    return pl.pallas_call(
        _proj_kernel,
        grid=(ni, nj),
        in_specs=[
            pl.BlockSpec((tm, k), lambda i, j: (i, 0)),
            pl.BlockSpec((None, k, tn), lambda i, j: (layer, 0, j)),
            pl.BlockSpec((None, rows, n2), lambda i, j: (layer, blk(i, j), 0)),
        ],
        out_specs=[
            pl.BlockSpec((tm, tn), lambda i, j: (i, j)),
            pl.BlockSpec((rows, n2), lambda i, j: (blk(i, j), 0)),
        ],
        out_shape=[jax.ShapeDtypeStruct((m, n), out_dtype), jax.ShapeDtypeStruct((k2, n2), BF16)],
        compiler_params=_cparams("arbitrary", "arbitrary"),
        name="proj",
    )(a, w, w_out)


def _gate_rows(gate_ref, tm, n_lat):
    rows = pl.program_id(0) * tm + lax.broadcasted_iota(jnp.int32, (tm, 1), 0)
    return jnp.where(rows < n_lat, gate_ref[0:1, :], gate_ref[1:2, :])


def _mix_out_kernel(a0_ref, a1_ref, a2_ref, w_ref, res_ref, gate_ref, o_ref, *, tm, n_lat):
    a = jnp.concatenate([a0_ref[...], a1_ref[...], a2_ref[...]], axis=1)
    acc = jnp.dot(a, w_ref[...], preferred_element_type=F32)
    o_ref[...] = res_ref[...] + _gate_rows(gate_ref, tm, n_lat) * acc


def _mix_out(a0, a1, a2, w, res, gate, n_lat, m, tm, tn):
    k, n = w.shape
    return pl.pallas_call(
        functools.partial(_mix_out_kernel, tm=tm, n_lat=n_lat),
        grid=(m // tm, n // tn),
        in_specs=[
            pl.BlockSpec((tm, a0.shape[1]), lambda i, j: (i, 0)),
            pl.BlockSpec((tm, a1.shape[1]), lambda i, j: (i, 0)),
            pl.BlockSpec((tm, a2.shape[1]), lambda i, j: (i, 0)),
            pl.BlockSpec((k, tn), lambda i, j: (0, j)),
            pl.BlockSpec((tm, tn), lambda i, j: (i, j)),
            pl.BlockSpec((2, tn), lambda i, j: (0, j)),
        ],
        out_specs=pl.BlockSpec((tm, tn), lambda i, j: (i, j)),
        out_shape=jax.ShapeDtypeStruct((m, n), F32),
        compiler_params=_cparams("parallel", "arbitrary"),
        name="mix_out",
    )(a0, a1, a2, w, res, gate)


def _ffn_up_kernel(a_ref, w1_ref, w3_ref, w2_ref, o_ref, w2b_ref):
    w2b_ref[...] = w2_ref[...].astype(BF16)
    t = jnp.dot(a_ref[...], w1_ref[...].astype(BF16), preferred_element_type=F32)
    u = jnp.dot(a_ref[...], w3_ref[...].astype(BF16), preferred_element_type=F32)
    o_ref[...] = (_silu(t) * u).astype(o_ref.dtype)


def _ffn_up(a, w1, w3, w2, layer, tm, tn):
    m, k = a.shape
    n = w1.shape[2]
    n2 = w2.shape[2]
    ni, nj = m // tm, n // tn
    rows = n // (ni * nj)
    assert rows * ni * nj == n and rows % (2 * SUBLANES) == 0
    return pl.pallas_call(
        _ffn_up_kernel,
        grid=(ni, nj),
        in_specs=[
            pl.BlockSpec((tm, k), lambda i, j: (i, 0)),
            pl.BlockSpec((None, k, tn), lambda i, j: (layer, 0, j)),
            pl.BlockSpec((None, k, tn), lambda i, j: (layer, 0, j)),
            pl.BlockSpec((None, rows, n2), lambda i, j: (layer, i * nj + j, 0)),
        ],
        out_specs=[
            pl.BlockSpec((tm, tn), lambda i, j: (i, j)),
            pl.BlockSpec((rows, n2), lambda i, j: (i * nj + j, 0)),
        ],
        out_shape=[jax.ShapeDtypeStruct((m, n), BF16), jax.ShapeDtypeStruct((n, n2), BF16)],
        compiler_params=_cparams("parallel", "arbitrary"),
        name="ffn_up",
    )(a, w1, w3, w2)


def _ffn_down_kernel(a_ref, w_ref, res_ref, gate_ref, o_ref, *, tm, n_lat):
    acc = jnp.dot(a_ref[...], w_ref[...], preferred_element_type=F32)
    o_ref[...] = res_ref[...] + _gate_rows(gate_ref, tm, n_lat) * acc


def _ffn_down(a, w, res, gate, n_lat, tm, tn):
    m, k = a.shape
    n = w.shape[1]
    return pl.pallas_call(
        functools.partial(_ffn_down_kernel, tm=tm, n_lat=n_lat),
        grid=(m // tm, n // tn),
        in_specs=[
            pl.BlockSpec((tm, k), lambda i, j: (i, 0)),
            pl.BlockSpec((k, tn), lambda i, j: (0, j)),
            pl.BlockSpec((tm, tn), lambda i, j: (i, j)),
            pl.BlockSpec((2, tn), lambda i, j: (0, j)),
        ],
        out_specs=pl.BlockSpec((tm, tn), lambda i, j: (i, j)),
        out_shape=jax.ShapeDtypeStruct((m, n), F32),
        compiler_params=_cparams("parallel", "arbitrary"),
        name="ffn_down",
    )(a, w, res, gate)


def _conv_kernel(main_ref, prev_ref, next_ref, cw_ref, cb_ref, lng_ref, lnb_ref, pw_ref, pwb_ref,
                 o_ref, uext_ref, yc_ref, wbf_ref, *, n_lat_tiles, n_tiles):
    i = pl.program_id(0)
    tl = main_ref.shape[0]
    ch = o_ref.shape[1]
    nblk = ch // LANES

    @pl.when(i == 0)
    def _():
        wbf_ref[...] = pw_ref[...].astype(BF16)

    def glu(a):
        return a[:, :ch] * _sigmoid(a[:, ch:])

    is_start = jnp.logical_or(i == 0, i == n_lat_tiles)
    is_end = jnp.logical_or(i == n_lat_tiles - 1, i == n_tiles - 1)
    u = glu(main_ref[...])
    up = jnp.where(is_start, 0.0, glu(prev_ref[...]))
    un = jnp.where(is_end, 0.0, glu(next_ref[...]))
    for c in range(nblk):
        sl = slice(c * LANES, (c + 1) * LANES)
        uext_ref[c, 0:HALO, :] = up[:, sl]
        uext_ref[c, HALO:HALO + tl, :] = u[:, sl]
        uext_ref[c, HALO + tl:2 * HALO + tl, :] = un[:, sl]

    def chan_block(c, carry):
        acc = jnp.zeros((tl, LANES), F32)
        for k in range(CONV_K):
            acc = acc + uext_ref[c, pl.ds(k + HALO - CONV_K // 2, tl), :] * cw_ref[c, k:k + 1, :]
        yc_ref[c] = acc + cb_ref[c]
        return carry

    lax.fori_loop(0, nblk, chan_block, 0)

    y = jnp.concatenate([yc_ref[c] for c in range(nblk)], axis=1)
    mu = jnp.mean(y, axis=-1, keepdims=True)
    yd = y - mu
    var = jnp.mean(yd * yd, axis=-1, keepdims=True)
    yn = yd * lax.rsqrt(var + EPS) * lng_ref[...] + lnb_ref[...]
    s = _silu(yn).astype(BF16)
    o_ref[...] = (jnp.dot(s, wbf_ref[...], preferred_element_type=F32) + pwb_ref[...]).astype(o_ref.dtype)


def _conv_module(z, conv_w, conv_b, ln_g, ln_b, pw_w, layer, pw_b, n_lat):
    m = z.shape[0]
    ch = conv_w.shape[1]
    tl = SEQ_TILE
    nblk = ch // LANES
    n_tiles = m // tl
    hb = tl // HALO
    cw = conv_w.reshape(CONV_K, nblk, LANES).transpose(1, 0, 2)
    cb = conv_b.reshape(nblk, 1, LANES)
    return pl.pallas_call(
        functools.partial(_conv_kernel, n_lat_tiles=n_lat // tl, n_tiles=n_tiles),
        grid=(n_tiles,),
        in_specs=[
            pl.BlockSpec((tl, 2 * ch), lambda i: (i, 0)),
            pl.BlockSpec((HALO, 2 * ch), lambda i: (jnp.maximum(i * hb - 1, 0), 0)),
            pl.BlockSpec((HALO, 2 * ch), lambda i: (jnp.minimum((i + 1) * hb, n_tiles * hb - 1), 0)),
            pl.BlockSpec((nblk, CONV_K, LANES), lambda i: (0, 0, 0)),
            pl.BlockSpec((nblk, 1, LANES), lambda i: (0, 0, 0)),
            pl.BlockSpec((1, ch), lambda i: (0, 0)),
            pl.BlockSpec((1, ch), lambda i: (0, 0)),
            pl.BlockSpec((None, ch, ch), lambda i: (layer, 0, 0)),
            pl.BlockSpec((1, ch), lambda i: (0, 0)),
        ],
        out_specs=pl.BlockSpec((tl, ch), lambda i: (i, 0)),
        out_shape=jax.ShapeDtypeStruct((m, ch), BF16),
        scratch_shapes=[
            pltpu.VMEM((nblk, tl + 2 * HALO, LANES), F32),
            pltpu.VMEM((nblk, tl, LANES), F32),
            pltpu.VMEM((ch, ch), BF16),
        ],
        compiler_params=_cparams("arbitrary"),
        name="conv_module",
    )(z, z, z, cw, cb, ln_g.reshape(1, ch), ln_b.reshape(1, ch), pw_w, pw_b.reshape(1, ch))


Q_ROWS = 4
BAND_ROWS = Q_ROWS + WIN_H
_BLOCK_CASES = (
    (0, lambda a: max(a - WIN_H // 2, 0)),
    (-WIN_H // 2, lambda a: a),
    (-WIN_H, lambda a: min(a + WIN_H // 2, Q_ROWS)),
)


def _bias_table_kernel(rpb_ref, o_ref):
    h = pl.program_id(0)
    qc = lax.broadcasted_iota(jnp.int32, (GRID_W, GRID_W), 0)
    kc = lax.broadcasted_iota(jnp.int32, (GRID_W, GRID_W), 1)
    qstart = jnp.clip(qc - WIN_W // 2, 0, GRID_W - WIN_W)
    valid = jnp.logical_and(kc >= qstart, kc < qstart + WIN_W)
    dc = jnp.where(valid, kc - qc + WIN_W - 1, -1)
    tabs = []
    for dr in range(2 * WIN_H - 1):
        t = jnp.full((GRID_W, GRID_W), NEG_INF, F32)
        for d in range(2 * WIN_W - 1):
            t = jnp.where(dc == d, rpb_ref[h, dr, d] * LOG2E, t)
        tabs.append(t)
    masked = jnp.full((GRID_W, GRID_W), NEG_INF, F32)
    for case, (band0, first_valid) in enumerate(_BLOCK_CASES):
        for a in range(Q_ROWS):
            blocks = []
            for i in range(BAND_ROWS):
                in_window = first_valid(a) <= i < first_valid(a) + WIN_H
                dr = band0 + i - a + WIN_H - 1
                blocks.append(tabs[dr] if in_window else masked)
            o_ref[0, case, a * GRID_W:(a + 1) * GRID_W, :] = jnp.concatenate(blocks, axis=1)


def _bias_table(rpb):
    heads = rpb.shape[0]
    shape = (len(_BLOCK_CASES), Q_ROWS * GRID_W, BAND_ROWS * GRID_W)
    return pl.pallas_call(
        _bias_table_kernel,
        grid=(heads,),
        in_specs=[pl.BlockSpec(memory_space=pltpu.SMEM)],
        out_specs=pl.BlockSpec((1,) + shape, lambda h: (h, 0, 0, 0)),
        out_shape=jax.ShapeDtypeStruct((heads,) + shape, F32),
        compiler_params=_cparams("parallel"),
        name="bias_table",
    )(rpb)


def _attn_kernel(q_ref, k_ref, v_ref, qg_ref, kg_ref, bias_ref, o_ref, kn_ref, vb_ref, *, rows):
    m = q_ref.shape[0]
    scale = HEAD_DIM ** -0.5
    band = BAND_ROWS * GRID_W
    nt = (((1,), (1,)), ((), ()))

    def rms(x, g):
        return x * lax.rsqrt(jnp.mean(x * x, axis=-1, keepdims=True) + EPS) * g

    def prep(c, carry):
        r = pl.multiple_of(c * SEQ_TILE, SEQ_TILE)
        kn_ref[pl.ds(r, SEQ_TILE), :] = rms(k_ref[pl.ds(r, SEQ_TILE), :], kg_ref[...]).astype(BF16)
        vb_ref[pl.ds(r, SEQ_TILE), :] = v_ref[pl.ds(r, SEQ_TILE), :].astype(BF16)
        return carry

    lax.fori_loop(0, m // SEQ_TILE, prep, 0, unroll=3)

    n_lat = rows * GRID_W
    kc = kn_ref[n_lat:m, :]
    vc = vb_ref[n_lat:m, :]

    qc = rms(q_ref[n_lat:m, :], qg_ref[...]).astype(BF16)
    s = lax.dot_general(qc, kc, nt, preferred_element_type=F32) * scale
    p = jnp.exp(s - jnp.max(s, axis=-1, keepdims=True))
    l = jnp.sum(p, axis=-1, keepdims=True)
    o_ref[n_lat:m, :] = (jnp.dot(p.astype(BF16), vc, preferred_element_type=F32) / l).astype(o_ref.dtype)

    n_blocks = rows // Q_ROWS
    nq = Q_ROWS * GRID_W

    def block(bq, carry):
        r = bq * Q_ROWS
        band0 = jnp.clip(r - WIN_H // 2, 0, rows - BAND_ROWS)
        case = jnp.where(bq == 0, 0, jnp.where(bq == n_blocks - 1, 2, 1))
        qs = pl.multiple_of(r * GRID_W, nq)
        ks = pl.multiple_of(band0 * GRID_W, (WIN_H // 2) * GRID_W)
        q = rms(q_ref[pl.ds(qs, nq), :], qg_ref[...]).astype(BF16)
        kb = kn_ref[pl.ds(ks, band), :]
        s_loc = lax.dot_general(q, kb, nt, preferred_element_type=F32) * (scale * LOG2E) + bias_ref[0, case]
        s_ctx = lax.dot_general(q, kc, nt, preferred_element_type=F32) * (scale * LOG2E)
        mx = jnp.maximum(jnp.max(s_loc, axis=-1, keepdims=True), jnp.max(s_ctx, axis=-1, keepdims=True))
        p_loc = jnp.exp2(s_loc - mx)
        p_ctx = jnp.exp2(s_ctx - mx)
        den = jnp.sum(p_loc, axis=-1, keepdims=True) + jnp.sum(p_ctx, axis=-1, keepdims=True)
        o = jnp.dot(p_loc.astype(BF16), vb_ref[pl.ds(ks, band), :], preferred_element_type=F32)
        o = o + jnp.dot(p_ctx.astype(BF16), vc, preferred_element_type=F32)
        o_ref[pl.ds(qs, nq), :] = (o / den).astype(o_ref.dtype)
        return carry

    lax.fori_loop(0, n_blocks, block, 0, unroll=8)


def _attention(z, q_g, k_g, bias_tab, n_lat, col0, width):
    m = z.shape[0]
    heads = width // HEAD_DIM
    rows = n_lat // GRID_W
    assert rows % Q_ROWS == 0 and rows >= 2 * BAND_ROWS and Q_ROWS % (WIN_H // 2) == 0
    cb = col0 // HEAD_DIM
    return pl.pallas_call(
        functools.partial(_attn_kernel, rows=rows),
        grid=(heads,),
        in_specs=[
            pl.BlockSpec((m, HEAD_DIM), lambda h: (0, cb + h)),
            pl.BlockSpec((m, HEAD_DIM), lambda h: (0, cb + heads + h)),
            pl.BlockSpec((m, HEAD_DIM), lambda h: (0, cb + 2 * heads + h)),
            pl.BlockSpec((1, HEAD_DIM), lambda h: (0, 0)),
            pl.BlockSpec((1, HEAD_DIM), lambda h: (0, 0)),
            pl.BlockSpec((1,) + bias_tab.shape[1:], lambda h: (h, 0, 0, 0)),
        ],
        out_specs=pl.BlockSpec((m, HEAD_DIM), lambda h: (0, h)),
        out_shape=jax.ShapeDtypeStruct((m, width), BF16),
        scratch_shapes=[pltpu.VMEM((m, HEAD_DIM), BF16), pltpu.VMEM((m, HEAD_DIM), BF16)],
        compiler_params=_cparams("parallel"),
        name="nbr_attention",
    )(z, z, z, q_g.reshape(1, HEAD_DIM), k_g.reshape(1, HEAD_DIM), bias_tab)


def _discretize_kernel(lr_ref, li_ref, ls_ref, br_ref, bi_ref, ar_ref, ai_ref, bbr_ref, bbi_ref):
    lr = lr_ref[0]
    li = li_ref[0]
    step = jnp.exp(ls_ref[0])
    mag = jnp.exp(lr * step)
    ang = li * step
    a_re = mag * jnp.cos(ang)
    a_im = mag * jnp.sin(ang)
    inv = 1.0 / (lr * lr + li * li)
    f_re = ((a_re - 1.0) * lr + a_im * li) * inv
    f_im = (a_im * lr - (a_re - 1.0) * li) * inv
    br = br_ref[0]
    bi = bi_ref[0]
    ar_ref[0] = a_re
    ai_ref[0] = a_im
    bbr_ref[0] = f_re * br - f_im * bi
    bbi_ref[0] = f_re * bi + f_im * br


def _discretize(lam_re, lam_im, log_step, b_re, b_im):
    n_dir, g, n = lam_re.shape
    p = b_re.shape[-1]
    rep = lambda a: jnp.repeat(a, p, axis=1)
    lr = rep(lam_re)
    li = rep(lam_im)
    ls = rep(jnp.broadcast_to(log_step[:, :, None], (n_dir, g, n)))
    br = b_re.transpose(0, 1, 3, 2).reshape(n_dir, g * p, n)
    bi = b_im.transpose(0, 1, 3, 2).reshape(n_dir, g * p, n)
    spec = pl.BlockSpec((1, g * p, n), lambda d: (d, 0, 0))
    shp = jax.ShapeDtypeStruct((n_dir, g * p, n), F32)
    return pl.pallas_call(
        _discretize_kernel,
        grid=(n_dir,),
        in_specs=[spec] * 5,
        out_specs=[spec] * 4,
        out_shape=[shp] * 4,
        compiler_params=_cparams("parallel"),
        name="s5_discretize",
    )(lr, li, ls, br, bi)


def _scan_weights(a_re, a_im, bbt_re, bbt_im, c_re, c_im):
    g, n = a_re.shape[0] // SSM_GROUP, a_re.shape[1]
    gb = LANES // SSM_GROUP
    nb = g // gb
    eye = jnp.eye(gb, dtype=F32)
    ar = a_re[::SSM_GROUP].reshape(g * n // LANES, LANES)
    ai = a_im[::SSM_GROUP].reshape(g * n // LANES, LANES)

    def in_mat(bbt):
        blk = bbt.reshape(nb, gb, SSM_GROUP, n)
        return jnp.einsum("bgpn,gh->bgphn", blk, eye).reshape(nb, LANES, gb * n)

    def out_mat(c):
        blk = c.reshape(nb, gb, SSM_GROUP, n)
        return jnp.einsum("bgpn,gh->bgnhp", blk, eye).reshape(nb, gb * n, LANES)

    wb = jnp.concatenate([in_mat(bbt_re), in_mat(bbt_im)], axis=2)
    wc = jnp.concatenate([out_mat(c_re), -out_mat(c_im)], axis=1)
    pieces = wb.shape[2] // PIECE
    wb = wb.reshape(nb, LANES, pieces, PIECE).transpose(0, 2, 1, 3).reshape(nb * pieces, LANES, PIECE)
    wc = wc.reshape(nb * pieces, PIECE, LANES)
    return ar, ai, wb.astype(BF16), wc.astype(BF16)


def _scan_kernel(u_ref, wb_ref, ar_ref, ai_ref, wc_ref, o_ref, xs0_ref, xs1_ref, xs2_ref, hs_ref, ub_ref, *,
                 reverse):
    s = pl.program_id(0)
    t_len = u_ref.shape[0]
    n_piece = wb_ref.shape[0]
    nb = ub_ref.shape[0]
    ppb = n_piece // nb
    n_re = ar_ref.shape[0]
    spb = n_re // nb
    steps = t_len // n_piece
    im0 = n_re * SCAN_PITCH
    bufs = (xs0_ref, xs1_ref, xs2_ref)

    @pl.when(s == 0)
    def _():
        for ref in bufs:
            ref[...] = jnp.zeros_like(ref)
        hs_ref[...] = jnp.zeros_like(hs_ref)

    u = u_ref[...]
    for b in range(nb):
        ub_ref[b] = u[:, b * LANES:(b + 1) * LANES].astype(BF16)

    def stages(nxt_ref, cur_ref, prv_ref):
        ar = ar_ref[...]
        ai = ai_ref[...]
        h_re = hs_ref[0]
        h_im = hs_ref[1]
        for j in range(n_piece):
            b = j // ppb
            q = j % ppb
            row0 = ((q // 2) * n_re + b * spb + (q % 2) * 2) * SCAN_PITCH
            hp = jnp.concatenate(
                [prv_ref[pl.ds(row0, t_len), :], prv_ref[pl.ds(row0 + SCAN_PITCH, t_len), :]], axis=1)
            yp = jnp.dot(hp.astype(BF16), wc_ref[j], preferred_element_type=F32)
            cols = slice(b * LANES, (b + 1) * LANES)
            o_ref[:, cols] = yp if q == 0 else o_ref[:, cols] + yp
            xb = jnp.dot(ub_ref[b], wb_ref[j], preferred_element_type=F32)
            nxt_ref[pl.ds(row0, t_len), :] = xb[:, :LANES]
            nxt_ref[pl.ds(row0 + SCAN_PITCH, t_len), :] = xb[:, LANES:]
            for jj in range(j * steps, (j + 1) * steps):
                t = (t_len - 1 - jj) if reverse else jj
                x_re = cur_ref[pl.ds(t, n_re, stride=SCAN_PITCH), :]
                x_im = cur_ref[pl.ds(im0 + t, n_re, stride=SCAN_PITCH), :]
                h_re, h_im = ar * h_re - ai * h_im + x_re, ar * h_im + ai * h_re + x_im
                cur_ref[pl.ds(t, n_re, stride=SCAN_PITCH), :] = h_re
                cur_ref[pl.ds(im0 + t, n_re, stride=SCAN_PITCH), :] = h_im
        hs_ref[0] = h_re
        hs_ref[1] = h_im

    phase = lax.rem(s, 3)
    for r in range(3):
        pl.when(phase == r)(functools.partial(stages, bufs[r], bufs[(r + 2) % 3], bufs[(r + 1) % 3]))


def _s5_out_kernel(yf_ref, yb_ref, u_ref, d_ref, gw_ref, gb_ref, o_ref, gwbf_ref):
    @pl.when(pl.program_id(0) == 0)
    def _():
        gwbf_ref[...] = gw_ref[...].astype(BF16)

    y = yf_ref[...] + yb_ref[...] + u_ref[...] * d_ref[...]
    g = jax.nn.gelu(y)
    gate = jnp.dot(g.astype(BF16), gwbf_ref[...], preferred_element_type=F32) + gb_ref[...]
    o_ref[...] = (g * _sigmoid(gate)).astype(o_ref.dtype)


def _s5_mixer(z, col0, ch, scan_w, d_skip, glu_w, layer, glu_b, n_lat):
    m = z.shape[0]
    t_len = SEQ_TILE
    assert m - n_lat == t_len
    n_chunks = m // t_len
    ctx_chunk = n_chunks - 1
    cb = col0 // ch
    ar, ai, wb, wc = scan_w
    nb = ch // LANES
    n_re = ar.shape[1]
    assert t_len % wb.shape[1] == 0

    def scan(direction, order, reverse, name):
        which = 2 * layer + direction

        def const_spec(a):
            nd = a.ndim - 1
            return pl.BlockSpec((None,) + a.shape[1:], lambda s: (which,) + (0,) * nd)

        first = lambda s: order(jnp.minimum(s, n_chunks - 1))
        third = lambda s: order(jnp.clip(s - 2, 0, n_chunks - 1))
        return pl.pallas_call(
            functools.partial(_scan_kernel, reverse=reverse),
            grid=(n_chunks + 2,),
            in_specs=[pl.BlockSpec((t_len, ch), lambda s: (first(s), cb))]
            + [const_spec(a) for a in (wb, ar, ai, wc)],
            out_specs=pl.BlockSpec((t_len, ch), lambda s: (third(s), 0)),
            out_shape=jax.ShapeDtypeStruct((m, ch), F32),
            scratch_shapes=[pltpu.VMEM((2 * n_re * SCAN_PITCH, LANES), F32)] * 3 + [
                pltpu.VMEM((2, n_re, LANES), F32),
                pltpu.VMEM((nb, t_len, LANES), BF16),
            ],
            compiler_params=_cparams("arbitrary"),
            name=name,
        )(z, wb, ar, ai, wc)

    y_f = scan(0, lambda k: jnp.where(k == 0, ctx_chunk, k - 1), False, "s5_scan_fwd")
    y_b = scan(1, lambda k: jnp.where(k == 0, ctx_chunk, ctx_chunk - k), True, "s5_scan_bwd")

    tm = m // 8
    return pl.pallas_call(
        _s5_out_kernel,
        grid=(m // tm,),
        in_specs=[
            pl.BlockSpec((tm, ch), lambda i: (i, 0)),
            pl.BlockSpec((tm, ch), lambda i: (i, 0)),
            pl.BlockSpec((tm, ch), lambda i: (i, cb)),
            pl.BlockSpec((1, ch), lambda i: (0, 0)),
            pl.BlockSpec((None, ch, ch), lambda i: (layer, 0, 0)),
            pl.BlockSpec((1, ch), lambda i: (0, 0)),
        ],
        out_specs=pl.BlockSpec((tm, ch), lambda i: (i, 0)),
        out_shape=jax.ShapeDtypeStruct((m, ch), BF16),
        scratch_shapes=[pltpu.VMEM((ch, ch), BF16)],
        compiler_params=_cparams("arbitrary"),
        name="s5_out",
    )(y_f, y_b, z, d_skip.reshape(1, ch), glu_w, glu_b.reshape(1, ch))


def _layer(xa, cc, n_lat, layer, last, w_mod, b_mod, g_norm1, g_norm2, w_in, conv_w, conv_b, conv_ln_g,
           conv_ln_b, conv_pw_w, conv_pw_b, q_norm_g, k_norm_g, rpb, scan_w, ssm_d, glu_w, glu_b, w_out,
           w_ff1, w_ff3, w_ff2):
    unstacked = isinstance(xa, tuple)
    m_all = sum(a.shape[0] for a in xa) if unstacked else xa.shape[0]
    d = w_in.shape[1]
    conv_ch = conv_w.shape[1]
    ssm_ch = ssm_d.shape[0]
    na_width = (w_in.shape[2] - 2 * conv_ch - ssm_ch) // 3
    o1 = 2 * conv_ch
    o4 = o1 + 3 * na_width
    m_out = n_lat if last else m_all
    tm_all = m_all // 8
    tm_out = m_out // 8

    mod = _modulation(cc, w_mod, layer, b_mod)[:2]
    sh1, sc1, g1, sh2, sc2, g2 = [mod[:, i * d:(i + 1) * d] for i in range(6)]

    if unstacked:
        a, xa = _stack_norm_mod(xa[0], xa[1], g_norm1, sh1, sc1)
    else:
        a = _norm_mod(xa, g_norm1, sh1, sc1, n_lat, m_all)
    z, w_out_b = _proj(a, w_in, w_out, layer, F32, tm_all, 512)

    y_conv = _conv_module(z, conv_w, conv_b, conv_ln_g, conv_ln_b, conv_pw_w, layer, conv_pw_b, n_lat)
    y_na = _attention(z, q_norm_g, k_norm_g, _bias_table(rpb), n_lat, o1, na_width)
    y_s5 = _s5_mixer(z, o4, ssm_ch, scan_w, ssm_d, glu_w, layer, glu_b, n_lat)

    xa = _mix_out(y_conv, y_na, y_s5, w_out_b, xa, g1, n_lat, m_out, tm_out, 512)
    a = _norm_mod(xa, g_norm2, sh2, sc2, n_lat, m_out)
    hidden, w2b = _ffn_up(a, w_ff1, w_ff3, w_ff2, layer, tm_out, 256)
    return _ffn_down(hidden, w2b, xa, g2, n_lat, tm_out // 2, 512)


def kernel(x, c, ctx, c_ctx, w_mod, b_mod, g_norm1, g_norm2, w_in, conv_w, conv_b, conv_ln_g, conv_ln_b,
           conv_pw_w, conv_pw_b, q_norm_g, k_norm_g, rpb, ssm_lambda_re, ssm_lambda_im, ssm_log_step,
           ssm_b_re, ssm_b_im, ssm_c_re, ssm_c_im, ssm_d, ssm_glu_w, ssm_glu_b, w_out, w_ff1, w_ff3, w_ff2):
    batch, seq, d = x.shape
    n_ctx = ctx.shape[1]
    assert batch == 1 and n_ctx == SEQ_TILE and seq % GRID_W == 0
    depth = w_mod.shape[0]
    xa = (x[0], ctx[0])
    cc = jnp.zeros((SUBLANES, d), F32).at[0].set(c[0]).at[1].set(c_ctx)
    flat = lambda a: a.reshape((-1,) + a.shape[2:])
    disc = _discretize(flat(ssm_lambda_re), flat(ssm_lambda_im), flat(ssm_log_step), flat(ssm_b_re),
                       flat(ssm_b_im))
    scan_w = jax.vmap(_scan_weights)(*disc, flat(ssm_c_re), flat(ssm_c_im))
    for l in range(depth):
        xa = _layer(xa, cc, seq, l, l == depth - 1, w_mod, b_mod[l], g_norm1[l], g_norm2[l], w_in, conv_w[l],
                    conv_b[l], conv_ln_g[l], conv_ln_b[l], conv_pw_w, conv_pw_b[l], q_norm_g[l], k_norm_g[l],
                    rpb[l], scan_w, ssm_d[l], ssm_glu_w, ssm_glu_b[l], w_out, w_ff1, w_ff3, w_ff2)
    return xa[None]
```

```python
import functools

import jax
import jax.numpy as jnp
from jax import lax
from jax.experimental import pallas as pl
from jax.experimental.pallas import tpu as pltpu

F32 = jnp.float32
BF16 = jnp.bfloat16

GRID_W = 64
WIN_H = 8
WIN_W = 16
HEAD_DIM = 128
CONV_K = 31
SSM_GROUP = 16
SSM_STATE = 64
EPS = 1e-6
NEG_INF = -1e30
LOG2E = 1.4426950408889634

LANES = 128
SUBLANES = 8
VMEM_LIMIT = 56 * 1024 * 1024

SEQ_TILE = 256
SCAN_PITCH = SEQ_TILE + SUBLANES // 2
PIECE = 2 * LANES
HALO = 16


def _cparams(*sem):
    return pltpu.CompilerParams(dimension_semantics=sem, vmem_limit_bytes=VMEM_LIMIT)


def _sigmoid(x):
    return 1.0 / (1.0 + jnp.exp(-x))


def _silu(x):
    return x * _sigmoid(x)


def _mod_kernel(c_ref, w_ref, b_ref, o_ref):
    s = _silu(c_ref[...]).astype(BF16)
    o_ref[...] = jnp.dot(s, w_ref[...].astype(BF16), preferred_element_type=F32) + b_ref[...]


def _modulation(cc, w_mod, layer, b_mod):
    _, d, n = w_mod.shape
    tn = 512
    return pl.pallas_call(
        _mod_kernel,
        grid=(n // tn,),
        in_specs=[
            pl.BlockSpec((SUBLANES, d), lambda j: (0, 0)),
            pl.BlockSpec((None, d, tn), lambda j: (layer, 0, j)),
            pl.BlockSpec((1, tn), lambda j: (0, j)),
        ],
        out_specs=pl.BlockSpec((SUBLANES, tn), lambda j: (0, j)),
        out_shape=jax.ShapeDtypeStruct((SUBLANES, n), F32),
        compiler_params=_cparams("parallel"),
        name="adaln_mod",
    )(cc, w_mod, b_mod.reshape(1, n))


def _norm_mod_kernel(x_ref, g_ref, sh_ref, sc_ref, o_ref, *, n_lat_tiles):
    i = pl.program_id(0)
    x = x_ref[...]
    y = x * lax.rsqrt(jnp.mean(x * x, axis=-1, keepdims=True) + EPS) * g_ref[...]
    row = jnp.where(i >= n_lat_tiles, 1, 0)
    sh = sh_ref[pl.ds(row, 1), :]
    sc = sc_ref[pl.ds(row, 1), :]
    o_ref[...] = (y * (1.0 + sc) + sh).astype(BF16)


def _stack_norm_mod_kernel(lat_ref, ctx_ref, g_ref, sh_ref, sc_ref, o_ref, xa_ref, *, n_lat_tiles):
    i = pl.program_id(0)

    def emit(x, row):
        xa_ref[...] = x
        y = x * lax.rsqrt(jnp.mean(x * x, axis=-1, keepdims=True) + EPS) * g_ref[...]
        o_ref[...] = (y * (1.0 + sc_ref[row:row + 1, :]) + sh_ref[row:row + 1, :]).astype(BF16)

    @pl.when(i < n_lat_tiles)
    def _():
        emit(lat_ref[...], 0)

    @pl.when(i >= n_lat_tiles)
    def _():
        emit(ctx_ref[...], 1)


def _stack_norm_mod(lat, ctx, g, sh, sc):
    n_lat, d = lat.shape
    tm = SEQ_TILE
    n_lat_tiles = n_lat // tm
    m = n_lat + ctx.shape[0]
    return pl.pallas_call(
        functools.partial(_stack_norm_mod_kernel, n_lat_tiles=n_lat_tiles),
        grid=(m // tm,),
        in_specs=[
            pl.BlockSpec((tm, d), lambda i: (jnp.minimum(i, n_lat_tiles - 1), 0)),
            pl.BlockSpec((tm, d), lambda i: (jnp.maximum(i - n_lat_tiles, 0), 0)),
            pl.BlockSpec((1, d), lambda i: (0, 0)),
            pl.BlockSpec((2, d), lambda i: (0, 0)),
            pl.BlockSpec((2, d), lambda i: (0, 0)),
        ],
        out_specs=[pl.BlockSpec((tm, d), lambda i: (i, 0)), pl.BlockSpec((tm, d), lambda i: (i, 0))],
        out_shape=[jax.ShapeDtypeStruct((m, d), BF16), jax.ShapeDtypeStruct((m, d), F32)],
        compiler_params=_cparams("parallel"),
        name="stack_norm_mod",
    )(lat, ctx, g.reshape(1, d), sh, sc)


def _norm_mod(x, g, sh, sc, n_lat, m):
    d = x.shape[1]
    tm = SEQ_TILE
    return pl.pallas_call(
        functools.partial(_norm_mod_kernel, n_lat_tiles=n_lat // tm),
        grid=(m // tm,),
        in_specs=[
            pl.BlockSpec((tm, d), lambda i: (i, 0)),
            pl.BlockSpec((1, d), lambda i: (0, 0)),
            pl.BlockSpec((2, d), lambda i: (0, 0)),
            pl.BlockSpec((2, d), lambda i: (0, 0)),
        ],
        out_specs=pl.BlockSpec((tm, d), lambda i: (i, 0)),
        out_shape=jax.ShapeDtypeStruct((m, d), BF16),
        compiler_params=_cparams("parallel"),
        name="norm_mod",
    )(x, g.reshape(1, d), sh, sc)


def _proj_kernel(a_ref, w_ref, wo_ref, o_ref, wob_ref):
    wob_ref[...] = wo_ref[...].astype(BF16)
    o_ref[...] = jnp.dot(a_ref[...], w_ref[...].astype(BF16),
                         preferred_element_type=F32).astype(o_ref.dtype)


def _proj(a, w, w_out, layer, out_dtype, tm, tn):
    m, k = a.shape
    n = w.shape[2]
    k2, n2 = w_out.shape[1:]
    ni, nj = m // tm, n // tn
    rows = 2 * SUBLANES
    while k2 // rows > ni * nj:
        rows *= 2
    n_slices = k2 // rows
    assert n_slices * rows == k2
    blk = lambda i, j: jnp.minimum(i * nj + j, n_slices - 1)
    return pl.pallas_call(
        _proj_kernel,
        grid=(ni, nj),
        in_specs=[
            pl.BlockSpec((tm, k), lambda i, j: (i, 0)),
            pl.BlockSpec((None, k, tn), lambda i, j: (layer, 0, j)),
            pl.BlockSpec((None, rows, n2), lambda i, j: (layer, blk(i, j), 0)),
        ],
        out_specs=[
            pl.BlockSpec((tm, tn), lambda i, j: (i, j)),
            pl.BlockSpec((rows, n2), lambda i, j: (blk(i, j), 0)),
        ],
        out_shape=[jax.ShapeDtypeStruct((m, n), out_dtype), jax.ShapeDtypeStruct((k2, n2), BF16)],
        compiler_params=_cparams("arbitrary", "arbitrary"),
        name="proj",
    )(a, w, w_out)


def _gate_rows(gate_ref, tm, n_lat):
    rows = pl.program_id(0) * tm + lax.broadcasted_iota(jnp.int32, (tm, 1), 0)
    return jnp.where(rows < n_lat, gate_ref[0:1, :], gate_ref[1:2, :])


def _mix_out_kernel(a0_ref, a1_ref, a2_ref, w_ref, res_ref, gate_ref, o_ref, *, tm, n_lat):
    a = jnp.concatenate([a0_ref[...], a1_ref[...], a2_ref[...]], axis=1)
    acc = jnp.dot(a, w_ref[...], preferred_element_type=F32)
    o_ref[...] = res_ref[...] + _gate_rows(gate_ref, tm, n_lat) * acc


def _mix_out(a0, a1, a2, w, res, gate, n_lat, m, tm, tn):
    k, n = w.shape
    return pl.pallas_call(
        functools.partial(_mix_out_kernel, tm=tm, n_lat=n_lat),
        grid=(m // tm, n // tn),
        in_specs=[
            pl.BlockSpec((tm, a0.shape[1]), lambda i, j: (i, 0)),
            pl.BlockSpec((tm, a1.shape[1]), lambda i, j: (i, 0)),
            pl.BlockSpec((tm, a2.shape[1]), lambda i, j: (i, 0)),
            pl.BlockSpec((k, tn), lambda i, j: (0, j)),
            pl.BlockSpec((tm, tn), lambda i, j: (i, j)),
            pl.BlockSpec((2, tn), lambda i, j: (0, j)),
        ],
        out_specs=pl.BlockSpec((tm, tn), lambda i, j: (i, j)),
        out_shape=jax.ShapeDtypeStruct((m, n), F32),
        compiler_params=_cparams("parallel", "arbitrary"),
        name="mix_out",
    )(a0, a1, a2, w, res, gate)


def _ffn_up_kernel(a_ref, w1_ref, w3_ref, w2_ref, o_ref, w2b_ref):
    w2b_ref[...] = w2_ref[...].astype(BF16)
    t = jnp.dot(a_ref[...], w1_ref[...].astype(BF16), preferred_element_type=F32)
    u = jnp.dot(a_ref[...], w3_ref[...].astype(BF16), preferred_element_type=F32)
    o_ref[...] = (_silu(t) * u).astype(o_ref.dtype)


def _ffn_up(a, w1, w3, w2, layer, tm, tn):
    m, k = a.shape
    n = w1.shape[2]
    n2 = w2.shape[2]
    ni, nj = m // tm, n // tn
    rows = n // (ni * nj)
    assert rows * ni * nj == n and rows % (2 * SUBLANES) == 0
    return pl.pallas_call(
        _ffn_up_kernel,
        grid=(ni, nj),
        in_specs=[
            pl.BlockSpec((tm, k), lambda i, j: (i, 0)),
            pl.BlockSpec((None, k, tn), lambda i, j: (layer, 0, j)),
            pl.BlockSpec((None, k, tn), lambda i, j: (layer, 0, j)),
            pl.BlockSpec((None, rows, n2), lambda i, j: (layer, i * nj + j, 0)),
        ],
        out_specs=[
            pl.BlockSpec((tm, tn), lambda i, j: (i, j)),
            pl.BlockSpec((rows, n2), lambda i, j: (i * nj + j, 0)),
        ],
        out_shape=[jax.ShapeDtypeStruct((m, n), BF16), jax.ShapeDtypeStruct((n, n2), BF16)],
        compiler_params=_cparams("parallel", "arbitrary"),
        name="ffn_up",
    )(a, w1, w3, w2)


def _ffn_down_kernel(a_ref, w_ref, res_ref, gate_ref, o_ref, *, tm, n_lat):
    acc = jnp.dot(a_ref[...], w_ref[...], preferred_element_type=F32)
    o_ref[...] = res_ref[...] + _gate_rows(gate_ref, tm, n_lat) * acc


def _ffn_down(a, w, res, gate, n_lat, tm, tn):
    m, k = a.shape
    n = w.shape[1]
    return pl.pallas_call(
        functools.partial(_ffn_down_kernel, tm=tm, n_lat=n_lat),
        grid=(m // tm, n // tn),
        in_specs=[
            pl.BlockSpec((tm, k), lambda i, j: (i, 0)),
            pl.BlockSpec((k, tn), lambda i, j: (0, j)),
            pl.BlockSpec((tm, tn), lambda i, j: (i, j)),
            pl.BlockSpec((2, tn), lambda i, j: (0, j)),
        ],
        out_specs=pl.BlockSpec((tm, tn), lambda i, j: (i, j)),
        out_shape=jax.ShapeDtypeStruct((m, n), F32),
        compiler_params=_cparams("parallel", "arbitrary"),
        name="ffn_down",
    )(a, w, res, gate)


def _conv_kernel(main_ref, prev_ref, next_ref, cw_ref, cb_ref, lng_ref, lnb_ref, pw_ref, pwb_ref,
                 o_ref, uext_ref, yc_ref, wbf_ref, *, n_lat_tiles, n_tiles):
    i = pl.program_id(0)
    tl = main_ref.shape[0]
    ch = o_ref.shape[1]
    nblk = ch // LANES

    @pl.when(i == 0)
    def _():
        wbf_ref[...] = pw_ref[...].astype(BF16)

    def glu(a):
        return a[:, :ch] * _sigmoid(a[:, ch:])

    is_start = jnp.logical_or(i == 0, i == n_lat_tiles)
    is_end = jnp.logical_or(i == n_lat_tiles - 1, i == n_tiles - 1)
    u = glu(main_ref[...])
    up = jnp.where(is_start, 0.0, glu(prev_ref[...]))
    un = jnp.where(is_end, 0.0, glu(next_ref[...]))
    for c in range(nblk):
        sl = slice(c * LANES, (c + 1) * LANES)
        uext_ref[c, 0:HALO, :] = up[:, sl]
        uext_ref[c, HALO:HALO + tl, :] = u[:, sl]
        uext_ref[c, HALO + tl:2 * HALO + tl, :] = un[:, sl]

    def chan_block(c, carry):
        acc = jnp.zeros((tl, LANES), F32)
        for k in range(CONV_K):
            acc = acc + uext_ref[c, pl.ds(k + HALO - CONV_K // 2, tl), :] * cw_ref[c, k:k + 1, :]
        yc_ref[c] = acc + cb_ref[c]
        return carry

    lax.fori_loop(0, nblk, chan_block, 0)

    y = jnp.concatenate([yc_ref[c] for c in range(nblk)], axis=1)
    mu = jnp.mean(y, axis=-1, keepdims=True)
    yd = y - mu
    var = jnp.mean(yd * yd, axis=-1, keepdims=True)
    yn = yd * lax.rsqrt(var + EPS) * lng_ref[...] + lnb_ref[...]
    s = _silu(yn).astype(BF16)
    o_ref[...] = (jnp.dot(s, wbf_ref[...], preferred_element_type=F32) + pwb_ref[...]).astype(o_ref.dtype)


def _conv_module(z, conv_w, conv_b, ln_g, ln_b, pw_w, layer, pw_b, n_lat):
    m = z.shape[0]
    ch = conv_w.shape[1]
    tl = SEQ_TILE
    nblk = ch // LANES
    n_tiles = m // tl
    hb = tl // HALO
    cw = conv_w.reshape(CONV_K, nblk, LANES).transpose(1, 0, 2)
    cb = conv_b.reshape(nblk, 1, LANES)
    return pl.pallas_call(
        functools.partial(_conv_kernel, n_lat_tiles=n_lat // tl, n_tiles=n_tiles),
        grid=(n_tiles,),
        in_specs=[
            pl.BlockSpec((tl, 2 * ch), lambda i: (i, 0)),
            pl.BlockSpec((HALO, 2 * ch), lambda i: (jnp.maximum(i * hb - 1, 0), 0)),
            pl.BlockSpec((HALO, 2 * ch), lambda i: (jnp.minimum((i + 1) * hb, n_tiles * hb - 1), 0)),
            pl.BlockSpec((nblk, CONV_K, LANES), lambda i: (0, 0, 0)),
            pl.BlockSpec((nblk, 1, LANES), lambda i: (0, 0, 0)),
            pl.BlockSpec((1, ch), lambda i: (0, 0)),
            pl.BlockSpec((1, ch), lambda i: (0, 0)),
            pl.BlockSpec((None, ch, ch), lambda i: (layer, 0, 0)),
            pl.BlockSpec((1, ch), lambda i: (0, 0)),
        ],
        out_specs=pl.BlockSpec((tl, ch), lambda i: (i, 0)),
        out_shape=jax.ShapeDtypeStruct((m, ch), BF16),
        scratch_shapes=[
            pltpu.VMEM((nblk, tl + 2 * HALO, LANES), F32),
            pltpu.VMEM((nblk, tl, LANES), F32),
            pltpu.VMEM((ch, ch), BF16),
        ],
        compiler_params=_cparams("arbitrary"),
        name="conv_module",
    )(z, z, z, cw, cb, ln_g.reshape(1, ch), ln_b.reshape(1, ch), pw_w, pw_b.reshape(1, ch))


Q_ROWS = 4
BAND_ROWS = Q_ROWS + WIN_H
_BLOCK_CASES = (
    (0, lambda a: max(a - WIN_H // 2, 0)),
    (-WIN_H // 2, lambda a: a),
    (-WIN_H, lambda a: min(a + WIN_H // 2, Q_ROWS)),
)


def _bias_table_kernel(rpb_ref, o_ref):
    h = pl.program_id(0)
    qc = lax.broadcasted_iota(jnp.int32, (GRID_W, GRID_W), 0)
    kc = lax.broadcasted_iota(jnp.int32, (GRID_W, GRID_W), 1)
    qstart = jnp.clip(qc - WIN_W // 2, 0, GRID_W - WIN_W)
    valid = jnp.logical_and(kc >= qstart, kc < qstart + WIN_W)
    dc = jnp.where(valid, kc - qc + WIN_W - 1, -1)
    tabs = []
    for dr in range(2 * WIN_H - 1):
        t = jnp.full((GRID_W, GRID_W), NEG_INF, F32)
        for d in range(2 * WIN_W - 1):
            t = jnp.where(dc == d, rpb_ref[h, dr, d] * LOG2E, t)
        tabs.append(t)
    masked = jnp.full((GRID_W, GRID_W), NEG_INF, F32)
    for case, (band0, first_valid) in enumerate(_BLOCK_CASES):
        for a in range(Q_ROWS):
            blocks = []
            for i in range(BAND_ROWS):
                in_window = first_valid(a) <= i < first_valid(a) + WIN_H
                dr = band0 + i - a + WIN_H - 1
                blocks.append(tabs[dr] if in_window else masked)
            o_ref[0, case, a * GRID_W:(a + 1) * GRID_W, :] = jnp.concatenate(blocks, axis=1)


def _bias_table(rpb):
    heads = rpb.shape[0]
    shape = (len(_BLOCK_CASES), Q_ROWS * GRID_W, BAND_ROWS * GRID_W)
    return pl.pallas_call(
        _bias_table_kernel,
        grid=(heads,),
        in_specs=[pl.BlockSpec(memory_space=pltpu.SMEM)],
        out_specs=pl.BlockSpec((1,) + shape, lambda h: (h, 0, 0, 0)),
        out_shape=jax.ShapeDtypeStruct((heads,) + shape, F32),
        compiler_params=_cparams("parallel"),
        name="bias_table",
    )(rpb)


def _attn_kernel(q_ref, k_ref, v_ref, qg_ref, kg_ref, bias_ref, o_ref, kn_ref, vb_ref, *, rows):
    m = q_ref.shape[0]
    scale = HEAD_DIM ** -0.5
    band = BAND_ROWS * GRID_W
    nt = (((1,), (1,)), ((), ()))

    def rms(x, g):
        return x * lax.rsqrt(jnp.mean(x * x, axis=-1, keepdims=True) + EPS) * g

    def prep(c, carry):
        r = pl.multiple_of(c * SEQ_TILE, SEQ_TILE)
        kn_ref[pl.ds(r, SEQ_TILE), :] = rms(k_ref[pl.ds(r, SEQ_TILE), :], kg_ref[...]).astype(BF16)
        vb_ref[pl.ds(r, SEQ_TILE), :] = v_ref[pl.ds(r, SEQ_TILE), :].astype(BF16)
        return carry

    lax.fori_loop(0, m // SEQ_TILE, prep, 0, unroll=3)

    n_lat = rows * GRID_W
    kc = kn_ref[n_lat:m, :]
    vc = vb_ref[n_lat:m, :]

    qc = rms(q_ref[n_lat:m, :], qg_ref[...]).astype(BF16)
    s = lax.dot_general(qc, kc, nt, preferred_element_type=F32) * scale
    p = jnp.exp(s - jnp.max(s, axis=-1, keepdims=True))
    l = jnp.sum(p, axis=-1, keepdims=True)
    o_ref[n_lat:m, :] = (jnp.dot(p.astype(BF16), vc, preferred_element_type=F32) / l).astype(o_ref.dtype)

    n_blocks = rows // Q_ROWS
    nq = Q_ROWS * GRID_W

    def block(bq, carry):
        r = bq * Q_ROWS
        band0 = jnp.clip(r - WIN_H // 2, 0, rows - BAND_ROWS)
        case = jnp.where(bq == 0, 0, jnp.where(bq == n_blocks - 1, 2, 1))
        qs = pl.multiple_of(r * GRID_W, nq)
        ks = pl.multiple_of(band0 * GRID_W, (WIN_H // 2) * GRID_W)
        q = rms(q_ref[pl.ds(qs, nq), :], qg_ref[...]).astype(BF16)
        kb = kn_ref[pl.ds(ks, band), :]
        s_loc = lax.dot_general(q, kb, nt, preferred_element_type=F32) * (scale * LOG2E) + bias_ref[0, case]
        s_ctx = lax.dot_general(q, kc, nt, preferred_element_type=F32) * (scale * LOG2E)
        mx = jnp.maximum(jnp.max(s_loc, axis=-1, keepdims=True), jnp.max(s_ctx, axis=-1, keepdims=True))
        p_loc = jnp.exp2(s_loc - mx)
        p_ctx = jnp.exp2(s_ctx - mx)
        den = jnp.sum(p_loc, axis=-1, keepdims=True) + jnp.sum(p_ctx, axis=-1, keepdims=True)
        o = jnp.dot(p_loc.astype(BF16), vb_ref[pl.ds(ks, band), :], preferred_element_type=F32)
        o = o + jnp.dot(p_ctx.astype(BF16), vc, preferred_element_type=F32)
        o_ref[pl.ds(qs, nq), :] = (o / den).astype(o_ref.dtype)
        return carry

    lax.fori_loop(0, n_blocks, block, 0, unroll=8)


def _attention(z, q_g, k_g, bias_tab, n_lat, col0, width):
    m = z.shape[0]
    heads = width // HEAD_DIM
    rows = n_lat // GRID_W
    assert rows % Q_ROWS == 0 and rows >= 2 * BAND_ROWS and Q_ROWS % (WIN_H // 2) == 0
    cb = col0 // HEAD_DIM
    return pl.pallas_call(
        functools.partial(_attn_kernel, rows=rows),
        grid=(heads,),
        in_specs=[
            pl.BlockSpec((m, HEAD_DIM), lambda h: (0, cb + h)),
            pl.BlockSpec((m, HEAD_DIM), lambda h: (0, cb + heads + h)),
            pl.BlockSpec((m, HEAD_DIM), lambda h: (0, cb + 2 * heads + h)),
            pl.BlockSpec((1, HEAD_DIM), lambda h: (0, 0)),
            pl.BlockSpec((1, HEAD_DIM), lambda h: (0, 0)),
            pl.BlockSpec((1,) + bias_tab.shape[1:], lambda h: (h, 0, 0, 0)),
        ],
        out_specs=pl.BlockSpec((m, HEAD_DIM), lambda h: (0, h)),
        out_shape=jax.ShapeDtypeStruct((m, width), BF16),
        scratch_shapes=[pltpu.VMEM((m, HEAD_DIM), BF16), pltpu.VMEM((m, HEAD_DIM), BF16)],
        compiler_params=_cparams("parallel"),
        name="nbr_attention",
    )(z, z, z, q_g.reshape(1, HEAD_DIM), k_g.reshape(1, HEAD_DIM), bias_tab)


def _discretize_kernel(lr_ref, li_ref, ls_ref, br_ref, bi_ref, ar_ref, ai_ref, bbr_ref, bbi_ref):
    lr = lr_ref[0]
    li = li_ref[0]
    step = jnp.exp(ls_ref[0])
    mag = jnp.exp(lr * step)
    ang = li * step
    a_re = mag * jnp.cos(ang)
    a_im = mag * jnp.sin(ang)
    inv = 1.0 / (lr * lr + li * li)
    f_re = ((a_re - 1.0) * lr + a_im * li) * inv
    f_im = (a_im * lr - (a_re - 1.0) * li) * inv
    br = br_ref[0]
    bi = bi_ref[0]
    ar_ref[0] = a_re
    ai_ref[0] = a_im
    bbr_ref[0] = f_re * br - f_im * bi
    bbi_ref[0] = f_re * bi + f_im * br


def _discretize(lam_re, lam_im, log_step, b_re, b_im):
    n_dir, g, n = lam_re.shape
    p = b_re.shape[-1]
    rep = lambda a: jnp.repeat(a, p, axis=1)
    lr = rep(lam_re)
    li = rep(lam_im)
    ls = rep(jnp.broadcast_to(log_step[:, :, None], (n_dir, g, n)))
    br = b_re.transpose(0, 1, 3, 2).reshape(n_dir, g * p, n)
    bi = b_im.transpose(0, 1, 3, 2).reshape(n_dir, g * p, n)
    spec = pl.BlockSpec((1, g * p, n), lambda d: (d, 0, 0))
    shp = jax.ShapeDtypeStruct((n_dir, g * p, n), F32)
    return pl.pallas_call(
        _discretize_kernel,
        grid=(n_dir,),
        in_specs=[spec] * 5,
        out_specs=[spec] * 4,
        out_shape=[shp] * 4,
        compiler_params=_cparams("parallel"),
        name="s5_discretize",
    )(lr, li, ls, br, bi)


def _scan_weights(a_re, a_im, bbt_re, bbt_im, c_re, c_im):
    g, n = a_re.shape[0] // SSM_GROUP, a_re.shape[1]
    gb = LANES // SSM_GROUP
    nb = g // gb
    eye = jnp.eye(gb, dtype=F32)
    ar = a_re[::SSM_GROUP].reshape(g * n // LANES, LANES)
    ai = a_im[::SSM_GROUP].reshape(g * n // LANES, LANES)

    def in_mat(bbt):
        blk = bbt.reshape(nb, gb, SSM_GROUP, n)
        return jnp.einsum("bgpn,gh->bgphn", blk, eye).reshape(nb, LANES, gb * n)

    def out_mat(c):
        blk = c.reshape(nb, gb, SSM_GROUP, n)
        return jnp.einsum("bgpn,gh->bgnhp", blk, eye).reshape(nb, gb * n, LANES)

    wb = jnp.concatenate([in_mat(bbt_re), in_mat(bbt_im)], axis=2)
    wc = jnp.concatenate([out_mat(c_re), -out_mat(c_im)], axis=1)
    pieces = wb.shape[2] // PIECE
    wb = wb.reshape(nb, LANES, pieces, PIECE).transpose(0, 2, 1, 3).reshape(nb * pieces, LANES, PIECE)
    wc = wc.reshape(nb * pieces, PIECE, LANES)
    return ar, ai, wb.astype(BF16), wc.astype(BF16)


def _scan_kernel(u_ref, wb_ref, ar_ref, ai_ref, wc_ref, o_ref, xs0_ref, xs1_ref, xs2_ref, hs_ref, ub_ref, *,
                 reverse):
    s = pl.program_id(0)
    t_len = u_ref.shape[0]
    n_piece = wb_ref.shape[0]
    nb = ub_ref.shape[0]
    ppb = n_piece // nb
    n_re = ar_ref.shape[0]
    spb = n_re // nb
    steps = t_len // n_piece
    im0 = n_re * SCAN_PITCH
    bufs = (xs0_ref, xs1_ref, xs2_ref)

    @pl.when(s == 0)
    def _():
        for ref in bufs:
            ref[...] = jnp.zeros_like(ref)
        hs_ref[...] = jnp.zeros_like(hs_ref)

    u = u_ref[...]
    for b in range(nb):
        ub_ref[b] = u[:, b * LANES:(b + 1) * LANES].astype(BF16)

    def stages(nxt_ref, cur_ref, prv_ref):
        ar = ar_ref[...]
        ai = ai_ref[...]
        h_re = hs_ref[0]
        h_im = hs_ref[1]
        for j in range(n_piece):
            b = j // ppb
            q = j % ppb
            row0 = ((q // 2) * n_re + b * spb + (q % 2) * 2) * SCAN_PITCH
            hp = jnp.concatenate(
                [prv_ref[pl.ds(row0, t_len), :], prv_ref[pl.ds(row0 + SCAN_PITCH, t_len), :]], axis=1)
            yp = jnp.dot(hp.astype(BF16), wc_ref[j], preferred_element_type=F32)
            cols = slice(b * LANES, (b + 1) * LANES)
            o_ref[:, cols] = yp if q == 0 else o_ref[:, cols] + yp
            xb = jnp.dot(ub_ref[b], wb_ref[j], preferred_element_type=F32)
            nxt_ref[pl.ds(row0, t_len), :] = xb[:, :LANES]
            nxt_ref[pl.ds(row0 + SCAN_PITCH, t_len), :] = xb[:, LANES:]
            for jj in range(j * steps, (j + 1) * steps):
                t = (t_len - 1 - jj) if reverse else jj
                x_re = cur_ref[pl.ds(t, n_re, stride=SCAN_PITCH), :]
                x_im = cur_ref[pl.ds(im0 + t, n_re, stride=SCAN_PITCH), :]
                h_re, h_im = ar * h_re - ai * h_im + x_re, ar * h_im + ai * h_re + x_im
                cur_ref[pl.ds(t, n_re, stride=SCAN_PITCH), :] = h_re
                cur_ref[pl.ds(im0 + t, n_re, stride=SCAN_PITCH), :] = h_im
        hs_ref[0] = h_re
        hs_ref[1] = h_im

    phase = lax.rem(s, 3)
    for r in range(3):
        pl.when(phase == r)(functools.partial(stages, bufs[r], bufs[(r + 2) % 3], bufs[(r + 1) % 3]))


def _s5_out_kernel(yf_ref, yb_ref, u_ref, d_ref, gw_ref, gb_ref, o_ref, gwbf_ref):
    @pl.when(pl.program_id(0) == 0)
    def _():
        gwbf_ref[...] = gw_ref[...].astype(BF16)

    y = yf_ref[...] + yb_ref[...] + u_ref[...] * d_ref[...]
    g = jax.nn.gelu(y)
    gate = jnp.dot(g.astype(BF16), gwbf_ref[...], preferred_element_type=F32) + gb_ref[...]
    o_ref[...] = (g * _sigmoid(gate)).astype(o_ref.dtype)


def _s5_mixer(z, col0, ch, scan_w, d_skip, glu_w, layer, glu_b, n_lat):
    m = z.shape[0]
    t_len = SEQ_TILE
    assert m - n_lat == t_len
    n_chunks = m // t_len
    ctx_chunk = n_chunks - 1
    cb = col0 // ch
    ar, ai, wb, wc = scan_w
    nb = ch // LANES
    n_re = ar.shape[1]
    assert t_len % wb.shape[1] == 0

    def scan(direction, order, reverse, name):
        which = 2 * layer + direction

        def const_spec(a):
            nd = a.ndim - 1
            return pl.BlockSpec((None,) + a.shape[1:], lambda s: (which,) + (0,) * nd)

        first = lambda s: order(jnp.minimum(s, n_chunks - 1))
        third = lambda s: order(jnp.clip(s - 2, 0, n_chunks - 1))
        return pl.pallas_call(
            functools.partial(_scan_kernel, reverse=reverse),
            grid=(n_chunks + 2,),
            in_specs=[pl.BlockSpec((t_len, ch), lambda s: (first(s), cb))]
            + [const_spec(a) for a in (wb, ar, ai, wc)],
            out_specs=pl.BlockSpec((t_len, ch), lambda s: (third(s), 0)),
            out_shape=jax.ShapeDtypeStruct((m, ch), F32),
            scratch_shapes=[pltpu.VMEM((2 * n_re * SCAN_PITCH, LANES), F32)] * 3 + [
                pltpu.VMEM((2, n_re, LANES), F32),
                pltpu.VMEM((nb, t_len, LANES), BF16),
            ],
            compiler_params=_cparams("arbitrary"),
            name=name,
        )(z, wb, ar, ai, wc)

    y_f = scan(0, lambda k: jnp.where(k == 0, ctx_chunk, k - 1), False, "s5_scan_fwd")
    y_b = scan(1, lambda k: jnp.where(k == 0, ctx_chunk, ctx_chunk - k), True, "s5_scan_bwd")

    tm = m // 8
    return pl.pallas_call(
        _s5_out_kernel,
        grid=(m // tm,),
        in_specs=[
            pl.BlockSpec((tm, ch), lambda i: (i, 0)),
            pl.BlockSpec((tm, ch), lambda i: (i, 0)),
            pl.BlockSpec((tm, ch), lambda i: (i, cb)),
            pl.BlockSpec((1, ch), lambda i: (0, 0)),
            pl.BlockSpec((None, ch, ch), lambda i: (layer, 0, 0)),
            pl.BlockSpec((1, ch), lambda i: (0, 0)),
        ],
        out_specs=pl.BlockSpec((tm, ch), lambda i: (i, 0)),
        out_shape=jax.ShapeDtypeStruct((m, ch), BF16),
        scratch_shapes=[pltpu.VMEM((ch, ch), BF16)],
        compiler_params=_cparams("arbitrary"),
        name="s5_out",
    )(y_f, y_b, z, d_skip.reshape(1, ch), glu_w, glu_b.reshape(1, ch))


def _layer(xa, cc, n_lat, layer, last, w_mod, b_mod, g_norm1, g_norm2, w_in, conv_w, conv_b, conv_ln_g,
           conv_ln_b, conv_pw_w, conv_pw_b, q_norm_g, k_norm_g, rpb, scan_w, ssm_d, glu_w, glu_b, w_out,
           w_ff1, w_ff3, w_ff2):
    unstacked = isinstance(xa, tuple)
    m_all = sum(a.shape[0] for a in xa) if unstacked else xa.shape[0]
    d = w_in.shape[1]
    conv_ch = conv_w.shape[1]
    ssm_ch = ssm_d.shape[0]
    na_width = (w_in.shape[2] - 2 * conv_ch - ssm_ch) // 3
    o1 = 2 * conv_ch
    o4 = o1 + 3 * na_width
    m_out = n_lat if last else m_all
    tm_all = m_all // 8
    tm_out = m_out // 8

    mod = _modulation(cc, w_mod, layer, b_mod)[:2]
    sh1, sc1, g1, sh2, sc2, g2 = [mod[:, i * d:(i + 1) * d] for i in range(6)]

    if unstacked:
        a, xa = _stack_norm_mod(xa[0], xa[1], g_norm1, sh1, sc1)
    else:
        a = _norm_mod(xa, g_norm1, sh1, sc1, n_lat, m_all)
    z, w_out_b = _proj(a, w_in, w_out, layer, F32, tm_all, 512)

    y_conv = _conv_module(z, conv_w, conv_b, conv_ln_g, conv_ln_b, conv_pw_w, layer, conv_pw_b, n_lat)
    y_na = _attention(z, q_norm_g, k_norm_g, _bias_table(rpb), n_lat, o1, na_width)
    y_s5 = _s5_mixer(z, o4, ssm_ch, scan_w, ssm_d, glu_w, layer, glu_b, n_lat)

    xa = _mix_out(y_conv, y_na, y_s5, w_out_b, xa, g1, n_lat, m_out, tm_out, 512)
    a = _norm_mod(xa, g_norm2, sh2, sc2, n_lat, m_out)
    hidden, w2b = _ffn_up(a, w_ff1, w_ff3, w_ff2, layer, tm_out, 256)
    return _ffn_down(hidden, w2b, xa, g2, n_lat, tm_out // 2, 512)


def kernel(x, c, ctx, c_ctx, w_mod, b_mod, g_norm1, g_norm2, w_in, conv_w, conv_b, conv_ln_g, conv_ln_b,
           conv_pw_w, conv_pw_b, q_norm_g, k_norm_g, rpb, ssm_lambda_re, ssm_lambda_im, ssm_log_step,
           ssm_b_re, ssm_b_im, ssm_c_re, ssm_c_im, ssm_d, ssm_glu_w, ssm_glu_b, w_out, w_ff1, w_ff3, w_ff2):
    batch, seq, d = x.shape
    n_ctx = ctx.shape[1]
    assert batch == 1 and n_ctx == SEQ_TILE and seq % GRID_W == 0
    depth = w_mod.shape[0]
    xa = (x[0], ctx[0])
    cc = jnp.zeros((SUBLANES, d), F32).at[0].set(c[0]).at[1].set(c_ctx)
    flat = lambda a: a.reshape((-1,) + a.shape[2:])
    disc = _discretize(flat(ssm_lambda_re), flat(ssm_lambda_im), flat(ssm_log_step), flat(ssm_b_re),
                       flat(ssm_b_im))
    scan_w = jax.vmap(_scan_weights)(*disc, flat(ssm_c_re), flat(ssm_c_im))
    for l in range(depth):
        xa = _layer(xa, cc, seq, l, l == depth - 1, w_mod, b_mod[l], g_norm1[l], g_norm2[l], w_in, conv_w[l],
                    conv_b[l], conv_ln_g[l], conv_ln_b[l], conv_pw_w, conv_pw_b[l], q_norm_g[l], k_norm_g[l],
                    rpb[l], scan_w, ssm_d[l], ssm_glu_w, ssm_glu_b[l], w_out, w_ff1, w_ff3, w_ff2)
    return xa[None]
```
